```python
import jax, jax.numpy as jnp
from jax import lax
import numpy as np

D_MODEL = 1024
BATCH = 32
SEQ = 2048
DEPTH = 1
DEC_BATCH = 128
DEC_SEQ = 1
PAST_LEN = 8192
PAGE_SIZE = 128

SB_HEADS = 8
SB_HEAD_DIM = 64
SB_WIDTH = SB_HEADS * SB_HEAD_DIM
SB_SCALE = SB_HEAD_DIM ** -0.5
Q_BLOCK = 128
ML_HEADS = 4
ML_HEAD_DIM = 128
ML_WIDTH = ML_HEADS * ML_HEAD_DIM
ML_CHUNK = 64
ML_EPS = 1e-6
CONV_WIDTH = 4
MIX_WIDTH = SB_WIDTH + ML_WIDTH
IN_WIDTH = 3 * SB_WIDTH + 4 * ML_WIDTH + 2 * ML_HEADS
N_EXPERTS = 32
TOP_K = 4
D_EXPERT = D_MODEL
SWIGLU_LIMIT = 7.0
SWIGLU_ALPHA = 1.702
MOE_BLOCK = 128
NORM_EPS = 1e-6

kernel_name = "hymba_stickbreak_mlstm_moe_step"


def rms_norm(x, g):
    xf = x.astype(jnp.float32)
    y = xf * lax.rsqrt(jnp.mean(xf * xf, axis=-1, keepdims=True) + NORM_EPS)
    return (y * g.astype(jnp.float32)).astype(x.dtype)


def project_heads(xn, w_in, b_gates, q_norm_g, k_norm_g):
    b, s, _ = xn.shape
    p = jnp.einsum('bsd,de->bse', xn, w_in)
    o1, o2, o3 = SB_WIDTH, 2 * SB_WIDTH, 3 * SB_WIDTH
    sb_q = rms_norm(p[..., :o1].reshape(b, s, SB_HEADS, SB_HEAD_DIM), q_norm_g)
    sb_k = rms_norm(p[..., o1:o2].reshape(b, s, SB_HEADS, SB_HEAD_DIM), k_norm_g)
    sb_v = p[..., o2:o3].reshape(b, s, SB_HEADS, SB_HEAD_DIM)
    ml_qk = p[..., o3:o3 + 2 * ML_WIDTH]
    ml_v = p[..., o3 + 2 * ML_WIDTH:o3 + 3 * ML_WIDTH].reshape(b, s, ML_HEADS, ML_HEAD_DIM)
    ml_o = p[..., o3 + 3 * ML_WIDTH:o3 + 4 * ML_WIDTH]
    gates = p[..., o3 + 4 * ML_WIDTH:].astype(jnp.float32) + b_gates.astype(jnp.float32)
    ig = gates[..., :ML_HEADS]
    lf = jax.nn.log_sigmoid(gates[..., ML_HEADS:])
    return sb_q, sb_k, sb_v, ml_qk, ml_v, ml_o, ig, lf


def stick_breaking_weights(z, mask):
    log_stay = jnp.where(mask, jax.nn.log_sigmoid(-z), 0.0)
    rest = lax.cumsum(log_stay, axis=z.ndim - 1, reverse=True) - log_stay
    return jnp.where(mask, jnp.exp(jax.nn.log_sigmoid(z) + rest), 0.0)


def sb_prompt(q, k, v, sb_bias):
    b, s, h, dh = q.shape
    nb = s // Q_BLOCK
    qb = jnp.swapaxes(q.reshape(b, nb, Q_BLOCK, h, dh), 0, 1)
    k_pos = jnp.arange(s)
    bias = sb_bias.astype(jnp.float32)[None, :, None, None]

    def block(args):
        q_blk, start = args
        q_pos = start + jnp.arange(Q_BLOCK)
        z = jnp.einsum('bqhd,bkhd->bhqk', q_blk, k).astype(jnp.float32) * SB_SCALE + bias
        w = stick_breaking_weights(z, k_pos[None, :] < q_pos[:, None])
        return jnp.einsum('bhqk,bkhd->bqhd', w.astype(v.dtype), v)

    o = lax.map(block, (qb, jnp.arange(nb) * Q_BLOCK))
    return jnp.swapaxes(o, 0, 1).reshape(b, s, h, dh)


def sb_sample(q, k_new, v_new, k_past, v_past, sb_bias):
    t = q.shape[1]
    p = k_past.shape[1]
    bias = sb_bias.astype(jnp.float32)[None, :, None, None]
    z = jnp.concatenate([jnp.einsum('bqhd,bkhd->bhqk', q, k_past),
                         jnp.einsum('bqhd,bkhd->bhqk', q, k_new)], axis=-1).astype(jnp.float32) * SB_SCALE + bias
    q_pos = p + jnp.arange(t)
    k_pos = jnp.arange(p + t)
    w = stick_breaking_weights(z, k_pos[None, :] < q_pos[:, None]).astype(v_new.dtype)
    return (jnp.einsum('bhqk,bkhd->bqhd', w[..., :p], v_past)
            + jnp.einsum('bhqk,bkhd->bqhd', w[..., p:], v_new))


def causal_conv(u_ext, conv_w, conv_b):
    t = u_ext.shape[1] - (CONV_WIDTH - 1)
    out = u_ext[:, 0:t] * conv_w[0] + conv_b
    for w in range(1, CONV_WIDTH):
        out = out + u_ext[:, w:w + t] * conv_w[w]
    return out


def mlstm_qk(qk_conv):
    b, s, _ = qk_conv.shape
    a = jax.nn.silu(qk_conv)
    q = a[..., :ML_WIDTH].reshape(b, s, ML_HEADS, ML_HEAD_DIM)
    k = a[..., ML_WIDTH:].reshape(b, s, ML_HEADS, ML_HEAD_DIM) * (ML_HEAD_DIM ** -0.5)
    return q, k


def mlstm_chunk(carry, xs):
    c_in, n_in, m_in = carry
    q, k, v, ig, lf = xs
    L = q.shape[1]
    f32 = jnp.float32
    qf, kf, vf = q.astype(f32), k.astype(f32), v.astype(f32)
    ig = jnp.swapaxes(ig, 1, 2)
    bcum = lax.cumsum(jnp.swapaxes(lf, 1, 2), axis=2)
    causal = jnp.tril(jnp.ones((L, L), dtype=bool))
    log_d = jnp.where(causal, bcum[..., :, None] - bcum[..., None, :] + ig[..., None, :], -jnp.inf)
    m_prev = m_in.astype(f32)
    m_inter = bcum + m_prev[..., None]
    m_t = jnp.maximum(m_inter, jnp.max(log_d, axis=-1))
    d = jnp.exp(log_d - m_t[..., None])
    inter = jnp.exp(m_inter - m_t)
    c_f, n_f = c_in.astype(f32), n_in.astype(f32)
    w = jnp.einsum('blhd,bshd->bhls', qf, kf) * d
    num = jnp.einsum('bhls,bshe->bhle', w, vf) + inter[..., None] * jnp.einsum('blhd,bhde->bhle', qf, c_f)
    den = jnp.sum(w, axis=-1) + inter * jnp.einsum('blhd,bhd->bhl', qf, n_f)
    h = num / (jnp.maximum(jnp.abs(den), jnp.exp(-m_t))[..., None] + ML_EPS)
    m_new = m_t[..., -1]
    decay = jnp.exp(bcum[..., -1:] - bcum + ig - m_new[..., None])
    carry_scale = jnp.exp(bcum[..., -1] + m_prev - m_new)
    c_new = carry_scale[..., None, None] * c_f + jnp.einsum('bhs,bshd,bshe->bhde', decay, kf, vf)
    n_new = carry_scale[..., None] * n_f + jnp.einsum('bhs,bshd->bhd', decay, kf)
    h = jnp.swapaxes(h, 1, 2).astype(v.dtype)
    return (c_new.astype(c_in.dtype), n_new.astype(n_in.dtype), m_new.astype(m_in.dtype)), h


def mlstm_prompt(q, k, v, ig, lf):
    b, s = q.shape[:2]
    nc = s // ML_CHUNK

    def chunks(a):
        return jnp.swapaxes(a.reshape((b, nc, ML_CHUNK) + a.shape[2:]), 0, 1)

    init = (jnp.zeros((b, ML_HEADS, ML_HEAD_DIM, ML_HEAD_DIM), jnp.float32),
            jnp.zeros((b, ML_HEADS, ML_HEAD_DIM), jnp.float32),
            jnp.zeros((b, ML_HEADS), jnp.float32))
    carry, h = lax.scan(mlstm_chunk, init, (chunks(q), chunks(k), chunks(v), chunks(ig), chunks(lf)))
    return carry, jnp.swapaxes(h, 0, 1).reshape(b, s, ML_HEADS, ML_HEAD_DIM)


def merge_heads(sb_o, ml_h, ml_o, sb_out_g, ml_out_g, w_out):
    b, s = sb_o.shape[:2]
    sb = rms_norm(sb_o, sb_out_g).reshape(b, s, SB_WIDTH)
    ml = jax.nn.sigmoid(ml_o) * rms_norm(ml_h, ml_out_g).reshape(b, s, ML_WIDTH)
    return jnp.einsum('bse,ed->bsd', jnp.concatenate([sb, ml], axis=-1), w_out)


def moe_ffn(xn, w_router, b_router, w_up, b_up, w_down, b_down):
    t, d = xn.shape
    logits = jnp.einsum('td,de->te', xn.astype(jnp.float32), w_router.astype(jnp.float32)) + b_router.astype(jnp.float32)
    top_logit, top_idx = lax.top_k(logits, TOP_K)
    gates = jax.nn.softmax(top_logit, axis=-1).astype(xn.dtype)
    a = t * TOP_K
    e_flat = top_idx.reshape(a)
    tok_flat = jnp.repeat(jnp.arange(t, dtype=jnp.int32), TOP_K)
    g_flat = gates.reshape(a)
    order = jnp.argsort(e_flat)
    e_sorted = e_flat[order]
    counts = jnp.bincount(e_flat, length=N_EXPERTS)
    start = jnp.cumsum(counts) - counts
    padded = (counts + MOE_BLOCK - 1) // MOE_BLOCK * MOE_BLOCK
    pend = jnp.cumsum(padded)
    pstart = pend - padded
    dest = pstart[e_sorted] + (jnp.arange(a) - start[e_sorted])
    n_blocks = -(-a // MOE_BLOCK) + N_EXPERTS
    cap = n_blocks * MOE_BLOCK
    buf_tok = jnp.full((cap,), t, jnp.int32).at[dest].set(tok_flat[order])
    buf_gate = jnp.zeros((cap,), xn.dtype).at[dest].set(g_flat[order])
    block_expert = jnp.minimum(jnp.searchsorted(pend, jnp.arange(n_blocks) * MOE_BLOCK, side='right'), N_EXPERTS - 1)
    x_pad = jnp.concatenate([xn, jnp.zeros((1, d), xn.dtype)], axis=0)
    xb = x_pad[buf_tok].reshape(n_blocks, MOE_BLOCK, d)

    def expert_block(args):
        x_blk, e = args
        h = x_blk @ w_up[e] + b_up[e]
        g = jnp.minimum(h[..., :D_EXPERT], SWIGLU_LIMIT)
        lin = jnp.clip(h[..., D_EXPERT:], -SWIGLU_LIMIT, SWIGLU_LIMIT)
        act = (lin + 1.0) * g * jax.nn.sigmoid(SWIGLU_ALPHA * g)
        return act @ w_down[e] + b_down[e]

    yb = lax.map(expert_block, (xb, block_expert)).reshape(cap, d) * buf_gate[:, None]
    return jax.ops.segment_sum(yb, buf_tok, num_segments=t + 1)[:t]


def setup_inputs(seed: int = 0) -> dict:
    key = jax.random.key(seed)
    ks = jax.random.split(key, 28)
    f32 = jnp.float32
    n_pages = PAST_LEN // PAGE_SIZE
    n_phys = (5 * DEC_BATCH * n_pages) // 4

    def nrm(k, shape, scale=1.0):
        return jax.random.normal(k, shape, f32) * scale

    def gain(k, shape):
        return 1.0 + 0.01 * jax.random.normal(k, shape, f32)

    page_table = jax.random.permutation(ks[4], n_phys)[:DEC_BATCH * n_pages].reshape(DEC_BATCH, n_pages).astype(jnp.int32)
    f_bias = jnp.linspace(3.0, 6.0, ML_HEADS, dtype=f32)
    b_gates = jnp.concatenate([nrm(ks[10], (DEPTH, ML_HEADS), 0.1),
                               f_bias + nrm(ks[11], (DEPTH, ML_HEADS), 0.01)], axis=-1)
    sb_bias = jnp.linspace(-5.0, -8.0, SB_HEADS, dtype=f32) + nrm(ks[27], (DEPTH, SB_HEADS), 0.01)
    return {
        "x_prompt": nrm(ks[0], (BATCH, SEQ, D_MODEL)),
        "x_sample": nrm(ks[1], (DEC_BATCH, DEC_SEQ, D_MODEL)),
        "cache_sb_k": nrm(ks[2], (DEPTH, n_phys, PAGE_SIZE, SB_HEADS, SB_HEAD_DIM)),
        "cache_sb_v": nrm(ks[3], (DEPTH, n_phys, PAGE_SIZE, SB_HEADS, SB_HEAD_DIM)),
        "page_table": page_table,
        "state_mlstm_C": nrm(ks[5], (DEPTH, DEC_BATCH, ML_HEADS, ML_HEAD_DIM, ML_HEAD_DIM), 0.1),
        "state_mlstm_n": nrm(ks[6], (DEPTH, DEC_BATCH, ML_HEADS, ML_HEAD_DIM), 0.1),
        "state_mlstm_m": nrm(ks[7], (DEPTH, DEC_BATCH, ML_HEADS)),
        "state_mlstm_conv": nrm(ks[8], (DEPTH, DEC_BATCH, CONV_WIDTH - 1, 2 * ML_WIDTH)),
        "norm_mix_g": gain(ks[9], (DEPTH, D_MODEL)),
        "w_in": nrm(ks[12], (DEPTH, D_MODEL, IN_WIDTH), D_MODEL ** -0.5),
        "b_gates": b_gates,
        "q_norm_g": gain(ks[13], (DEPTH, SB_HEAD_DIM)),
        "k_norm_g": gain(ks[14], (DEPTH, SB_HEAD_DIM)),
        "sb_bias": sb_bias,
        "conv_w": nrm(ks[15], (DEPTH, CONV_WIDTH, 2 * ML_WIDTH), CONV_WIDTH ** -0.5),
        "conv_b": nrm(ks[16], (DEPTH, 2 * ML_WIDTH), 0.01),
        "sb_out_g": gain(ks[17], (DEPTH, SB_HEADS, SB_HEAD_DIM)),
        "ml_out_g": gain(ks[18], (DEPTH, ML_HEADS, ML_HEAD_DIM)),
        "w_out": nrm(ks[19], (DEPTH, MIX_WIDTH, D_MODEL), MIX_WIDTH ** -0.5),
        "norm_ffn_g": gain(ks[20], (DEPTH, D_MODEL)),
        "w_router": nrm(ks[21], (DEPTH, D_MODEL, N_EXPERTS), D_MODEL ** -0.5),
        "b_router": nrm(ks[22], (DEPTH, N_EXPERTS), 0.01),
        "w_up": nrm(ks[23], (DEPTH, N_EXPERTS, D_MODEL, 2 * D_EXPERT), D_MODEL ** -0.5),
        "b_up": nrm(ks[24], (DEPTH, N_EXPERTS, 2 * D_EXPERT), 0.01),
        "w_down": nrm(ks[25], (DEPTH, N_EXPERTS, D_EXPERT, D_MODEL), D_EXPERT ** -0.5),
        "b_down": nrm(ks[26], (DEPTH, N_EXPERTS, D_MODEL), 0.01),
    }


def reference(x_prompt, x_sample, cache_sb_k, cache_sb_v, page_table, state_mlstm_C, state_mlstm_n,
              state_mlstm_m, state_mlstm_conv, norm_mix_g, w_in, b_gates, q_norm_g, k_norm_g, sb_bias, conv_w, conv_b,
              sb_out_g, ml_out_g, w_out, norm_ffn_g, w_router, b_router, w_up, b_up, w_down, b_down):
    bp, sp, d = x_prompt.shape
    bs, ss, _ = x_sample.shape
    past = page_table.shape[1] * cache_sb_k.shape[2]
    xp, xs = x_prompt, x_sample
    new = {name: [] for name in ('kp', 'vp', 'ks', 'vs', 'cp', 'np', 'mp', 'convp', 'cs', 'ns', 'ms', 'convs')}
    for l in range(DEPTH):
        xn = rms_norm(xp, norm_mix_g[l])
        sb_q, sb_k, sb_v, ml_qk, ml_v, ml_o, ig, lf = project_heads(xn, w_in[l], b_gates[l], q_norm_g[l], k_norm_g[l])
        sb_o = sb_prompt(sb_q, sb_k, sb_v, sb_bias[l])
        qk_ext = jnp.pad(ml_qk, ((0, 0), (CONV_WIDTH - 1, 0), (0, 0)))
        ml_q, ml_k = mlstm_qk(causal_conv(qk_ext, conv_w[l], conv_b[l]))
        (c_p, n_p, m_p), ml_h = mlstm_prompt(ml_q, ml_k, ml_v, ig, lf)
        xp = xp + merge_heads(sb_o, ml_h, ml_o, sb_out_g[l], ml_out_g[l], w_out[l])
        new['kp'].append(sb_k)
        new['vp'].append(sb_v)
        new['cp'].append(c_p)
        new['np'].append(n_p)
        new['mp'].append(m_p)
        new['convp'].append(qk_ext[:, qk_ext.shape[1] - (CONV_WIDTH - 1):])
        xn = rms_norm(xs, norm_mix_g[l])
        sb_q, sb_k, sb_v, ml_qk, ml_v, ml_o, ig, lf = project_heads(xn, w_in[l], b_gates[l], q_norm_g[l], k_norm_g[l])
        k_past = cache_sb_k[l][page_table].reshape(bs, past, SB_HEADS, SB_HEAD_DIM).astype(sb_k.dtype)
        v_past = cache_sb_v[l][page_table].reshape(bs, past, SB_HEADS, SB_HEAD_DIM).astype(sb_v.dtype)
        sb_o = sb_sample(sb_q, sb_k, sb_v, k_past, v_past, sb_bias[l])
        qk_ext = jnp.concatenate([state_mlstm_conv[l].astype(ml_qk.dtype), ml_qk], axis=1)
        ml_q, ml_k = mlstm_qk(causal_conv(qk_ext, conv_w[l], conv_b[l]))
        (c_s, n_s, m_s), ml_h = mlstm_chunk((state_mlstm_C[l], state_mlstm_n[l], state_mlstm_m[l]),
                                            (ml_q, ml_k, ml_v, ig, lf))
        xs = xs + merge_heads(sb_o, ml_h, ml_o, sb_out_g[l], ml_out_g[l], w_out[l])
        new['ks'].append(sb_k)
        new['vs'].append(sb_v)
        new['cs'].append(c_s)
        new['ns'].append(n_s)
        new['ms'].append(m_s)
        new['convs'].append(qk_ext[:, qk_ext.shape[1] - (CONV_WIDTH - 1):])
        tok = jnp.concatenate([xp.reshape(bp * sp, d), xs.reshape(bs * ss, d)], axis=0)
        ffn = moe_ffn(rms_norm(tok, norm_ffn_g[l]), w_router[l], b_router[l], w_up[l], b_up[l], w_down[l], b_down[l])
        xp = xp + ffn[:bp * sp].reshape(bp, sp, d)
        xs = xs + ffn[bp * sp:].reshape(bs, ss, d)
    return (xp, xs,
            jnp.stack(new['kp']), jnp.stack(new['vp']), jnp.stack(new['ks']), jnp.stack(new['vs']),
            jnp.stack(new['cp']), jnp.stack(new['np']), jnp.stack(new['mp']), jnp.stack(new['convp']),
            jnp.stack(new['cs']), jnp.stack(new['ns']), jnp.stack(new['ms']), jnp.stack(new['convs']))
```

```python
import functools

import jax
import jax.numpy as jnp
from jax import lax
from jax.experimental import pallas as pl
from jax.experimental.pallas import tpu as pltpu

F32 = jnp.float32
BF16 = jnp.bfloat16
HIGHEST = lax.Precision.HIGHEST

SB_HEADS = 8
SB_HEAD_DIM = 64
SB_WIDTH = SB_HEADS * SB_HEAD_DIM
SB_SCALE = SB_HEAD_DIM ** -0.5
ML_HEADS = 4
ML_HEAD_DIM = 128
ML_WIDTH = ML_HEADS * ML_HEAD_DIM
ML_EPS = 1e-6
CONV_WIDTH = 4
N_EXPERTS = 32
TOP_K = 4
SWIGLU_LIMIT = 7.0
SWIGLU_ALPHA = 1.702
NORM_EPS = 1e-6
LOG2E = 1.4426950408889634

LANES = 128
SUBLANES = 8
VMEM_LIMIT_BYTES = 56 * 1024 * 1024

PROJ_ROWS = 256
SB_BLOCK = 256
ML_CHUNK = 128
DEC_PAGES_PER_STEP = 8
ROUTER_ROWS = 512
EXPERT_ROWS = 512
DISPATCH_ROWS = 512
COMBINE_ROWS = 256
ROW_DMA_UNROLL = 8
GATE_LANES = 2 * ML_HEADS


def _cparams(sem, vmem=VMEM_LIMIT_BYTES):
    return pltpu.CompilerParams(dimension_semantics=sem, vmem_limit_bytes=vmem)


def _softplus(z):
    return jnp.maximum(z, 0.0) + jnp.log(1.0 + jnp.exp2(jnp.abs(z) * (-LOG2E)))


def _exp(x):
    return jnp.exp2(x * LOG2E)


def _split_bf16(x):
    hi = x.astype(BF16)
    lo = (x - hi.astype(F32)).astype(BF16)
    return hi, lo


def _proj_kernel(x_ref, g_ref, w_ref, wg_ref, wgt_ref, bd_ref, qkg_ref, bg_ref, bgt_ref,
                 q_ref, k_ref, kb_ref, v_ref, vb_ref, mqk_ref, mv_ref, so_ref, gate_ref, gatet_ref):
    x = x_ref[...]
    ms = jnp.mean(x * x, axis=-1, keepdims=True)
    xn = (x * lax.rsqrt(ms + NORM_EPS)) * g_ref[...]
    xb = xn.astype(BF16)

    def proj(lo, hi):
        return jnp.dot(xb, w_ref[:, lo:hi], preferred_element_type=F32)

    o1, o2, o3 = SB_WIDTH, 2 * SB_WIDTH, 3 * SB_WIDTH
    pqk = proj(0, o2)
    sq_hi, sq_lo = _split_bf16(pqk * pqk)
    bd = bd_ref[...]
    slab = bd.shape[0]
    parts = []
    for s in range(o2 // slab):
        sl = slice(s * slab, (s + 1) * slab)
        parts.append(jnp.dot(sq_hi[:, sl], bd, preferred_element_type=F32)
                     + jnp.dot(sq_lo[:, sl], bd, preferred_element_type=F32))
    msq = jnp.concatenate(parts, axis=-1)
    qkn = (pqk * lax.rsqrt(msq + NORM_EPS)) * qkg_ref[...]
    q_ref[...] = (qkn[:, :o1] * SB_SCALE).astype(BF16)
    kn = qkn[:, o1:]
    k_ref[...] = kn
    kb_ref[...] = kn.astype(BF16)
    pv = proj(o2, o3)
    v_ref[...] = pv
    vb_ref[...] = pv.astype(BF16)
    mqk_ref[...] = proj(o3, o3 + 2 * ML_WIDTH)
    mv_ref[...] = proj(o3 + 2 * ML_WIDTH, o3 + 3 * ML_WIDTH).astype(BF16)
    so_ref[...] = jax.nn.sigmoid(proj(o3 + 3 * ML_WIDTH, o3 + 4 * ML_WIDTH))

    def gate_act(g, is_ig):
        lf = jnp.minimum(g, 0.0) - jnp.log(1.0 + jnp.exp(-jnp.abs(g)))
        return jnp.where(is_ig, g, lf)

    g = jnp.dot(xb, wg_ref[...], preferred_element_type=F32) + bg_ref[...]
    lane = lax.broadcasted_iota(jnp.int32, g.shape, 1)
    gate_ref[...] = gate_act(g, lane < ML_HEADS)
    gt = lax.dot_general(wgt_ref[...], xb, (((1,), (1,)), ((), ())),
                         preferred_element_type=F32) + bgt_ref[...]
    row = lax.broadcasted_iota(jnp.int32, gt.shape, 0)
    gatet_ref[...] = gate_act(gt, row < ML_HEADS)


def _proj(x2d, norm_g, w_main, w_gate, w_gate_t, bd, qk_gain, b_gate, b_gate_t, rows):
    t, d = x2d.shape
    n_main = w_main.shape[1]
    full = lambda shape: pl.BlockSpec(shape, lambda i: (0,) * len(shape))
    rowblk = lambda width: pl.BlockSpec((rows, width), lambda i: (i, 0))
    out_shape = (
        jax.ShapeDtypeStruct((t, SB_WIDTH), BF16),
        jax.ShapeDtypeStruct((t, SB_WIDTH), F32),
        jax.ShapeDtypeStruct((t, SB_WIDTH), BF16),
        jax.ShapeDtypeStruct((t, SB_WIDTH), F32),
        jax.ShapeDtypeStruct((t, SB_WIDTH), BF16),
        jax.ShapeDtypeStruct((t, 2 * ML_WIDTH), F32),
        jax.ShapeDtypeStruct((t, ML_WIDTH), BF16),
        jax.ShapeDtypeStruct((t, ML_WIDTH), F32),
        jax.ShapeDtypeStruct((t, LANES), F32),
        jax.ShapeDtypeStruct((2 * GATE_LANES, t), F32),
    )
    out_specs = (
        rowblk(SB_WIDTH), rowblk(SB_WIDTH), rowblk(SB_WIDTH), rowblk(SB_WIDTH), rowblk(SB_WIDTH),
        rowblk(2 * ML_WIDTH), rowblk(ML_WIDTH), rowblk(ML_WIDTH), rowblk(LANES),
        pl.BlockSpec((2 * GATE_LANES, rows), lambda i: (0, i)),
    )
    return pl.pallas_call(
        _proj_kernel,
        grid=(t // rows,),
        in_specs=[rowblk(d), full((1, d)), full((d, n_main)), full((d, LANES)),
                  full((2 * GATE_LANES, d)), full(bd.shape), full((1, 2 * SB_WIDTH)),
                  full((1, LANES)), full((2 * GATE_LANES, 1))],
        out_specs=out_specs,
        out_shape=out_shape,
        compiler_params=_cparams(("arbitrary",)),
        name="proj",
    )(x2d, norm_g, w_main, w_gate, w_gate_t, bd, qk_gain, b_gate, b_gate_t)


def _head_norm_pair(o, gain):
    lane = lax.broadcasted_iota(jnp.int32, o.shape, 1)
    first = lane < SB_HEAD_DIM
    sq = o * o
    ms0 = jnp.sum(jnp.where(first, sq, 0.0), axis=1, keepdims=True)
    ms1 = jnp.sum(jnp.where(first, 0.0, sq), axis=1, keepdims=True)
    ms = jnp.where(first, ms0, ms1) * (1.0 / SB_HEAD_DIM)
    return (o * lax.rsqrt(ms + NORM_EPS)) * gain


def _sb_prompt_kernel(bias_ref, q_ref, k_ref, v_ref, u_ref, g_ref, o_ref, acc_ref, carry_ref):
    hp = pl.program_id(1)
    i = pl.program_id(2)
    blk = q_ref.shape[1]
    q = q_ref[0].astype(F32)
    lane = lax.broadcasted_iota(jnp.int32, q.shape, 1)
    first = lane < SB_HEAD_DIM
    qs = jnp.concatenate([jnp.where(first, q, 0.0), jnp.where(first, 0.0, q)], axis=0).astype(BF16)
    rowc = lax.broadcasted_iota(jnp.int32, (2 * blk, 1), 0)
    bias = jnp.where(rowc < blk, bias_ref[2 * hp], bias_ref[2 * hp + 1])
    u = u_ref[...]
    rowi = lax.broadcasted_iota(jnp.int32, (2 * blk, blk), 0)
    coli = lax.broadcasted_iota(jnp.int32, (2 * blk, blk), 1)
    causal = coli < jnp.where(rowi >= blk, rowi - blk, rowi)

    acc_ref[...] = jnp.zeros_like(acc_ref)
    carry_ref[...] = jnp.zeros_like(carry_ref)

    def step(j, diag):
        start = pl.multiple_of(j * blk, blk)
        kblk = k_ref[0, pl.ds(start, blk), :]
        vblk = v_ref[0, pl.ds(start, blk), :]
        z = lax.dot_general(qs, kblk, (((1,), (1,)), ((), ())), preferred_element_type=F32) + bias
        sp = _softplus(z)
        if diag:
            sp = jnp.where(causal, sp, 0.0)
        cs = jnp.dot(sp.astype(BF16), u, preferred_element_type=F32)
        a = _exp((z - cs) - carry_ref[...])
        if diag:
            a = jnp.where(causal, a, 0.0)
        acc_ref[...] += jnp.dot(a.astype(BF16), vblk, preferred_element_type=F32)
        carry_ref[...] += cs[:, 0:1]

    step(i, True)

    def pair(jj, c):
        step(i - 1 - 2 * jj, False)
        step(i - 2 - 2 * jj, False)
        return c

    lax.fori_loop(0, lax.shift_right_logical(i, 1), pair, 0)

    @pl.when((i & 1) == 1)
    def _():
        step(0, False)

    o = jnp.where(first, acc_ref[0:blk, :], acc_ref[blk:2 * blk, :])
    o_ref[0] = _head_norm_pair(o, g_ref[0]).astype(BF16)


def _sb_prompt(bias, q, k, v, u, gain):
    b, s, _ = q.shape
    blk = SB_BLOCK
    grid = (b, SB_WIDTH // LANES, s // blk)
    grid_spec = pltpu.PrefetchScalarGridSpec(
        num_scalar_prefetch=1,
        grid=grid,
        in_specs=[
            pl.BlockSpec((1, blk, LANES), lambda bi, hp, i, *_: (bi, i, hp)),
            pl.BlockSpec((1, s, LANES), lambda bi, hp, i, *_: (bi, 0, hp)),
            pl.BlockSpec((1, s, LANES), lambda bi, hp, i, *_: (bi, 0, hp)),
            pl.BlockSpec((blk, blk), lambda bi, hp, i, *_: (0, 0)),
            pl.BlockSpec((1, 1, LANES), lambda bi, hp, i, *_: (hp, 0, 0)),
        ],
        out_specs=pl.BlockSpec((1, blk, LANES), lambda bi, hp, i, *_: (bi, i, hp)),
        scratch_shapes=[pltpu.VMEM((2 * blk, LANES), F32), pltpu.VMEM((2 * blk, 1), F32)],
    )
    return pl.pallas_call(
        _sb_prompt_kernel,
        grid_spec=grid_spec,
        out_shape=jax.ShapeDtypeStruct((b, s, SB_WIDTH), BF16),
        compiler_params=_cparams(("arbitrary", "arbitrary", "arbitrary")),
        name="sb_prompt",
    )(bias, q, k, v, u, gain)


def _sb_decode_kernel(pt_ref, q_ref, bias_ref, u_ref, tot_ref, g_ref, *refs):
    npg = DEC_PAGES_PER_STEP
    k_refs = refs[:npg]
    v_refs = refs[npg:2 * npg]
    o_ref = refs[2 * npg]
    acc_ref, carry_ref = refs[2 * npg + 1:]
    j = pl.program_id(1)
    rows = 2 * SUBLANES
    page_rows = k_refs[0].shape[1] * SB_HEADS

    @pl.when(j == 0)
    def _():
        acc_ref[...] = jnp.zeros_like(acc_ref)
        carry_ref[...] = jnp.zeros_like(carry_ref)

    q = q_ref[0]
    rowm = lax.broadcasted_iota(jnp.int32, (rows, page_rows), 0)
    lanem = lax.broadcasted_iota(jnp.int32, (rows, page_rows), 1)
    own = (lanem & (SB_HEADS - 1)) == rowm
    zs = []
    for g in range(npg):
        kp = k_refs[g][0].reshape(page_rows, SB_HEAD_DIM).astype(BF16)
        m = lax.dot_general(q, kp, (((1,), (1,)), ((), ())), preferred_element_type=F32)
        zs.append(jnp.sum(jnp.where(own, m, 0.0), axis=0, keepdims=True))
    zs.append(jnp.zeros((rows - npg, page_rows), F32))
    z = jnp.concatenate(zs, axis=0) + bias_ref[...]
    spb = _softplus(z).astype(BF16)
    cs = jnp.dot(spb, u_ref[...], preferred_element_type=F32)
    tot = jnp.dot(spb, tot_ref[...], preferred_element_type=F32)
    carry = carry_ref[...]
    acc = acc_ref[...]
    for g in range(npg):
        a = _exp((z[g:g + 1] - cs[g:g + 1]) - carry)
        am = jnp.where(own, jnp.broadcast_to(a, (rows, page_rows)), 0.0).astype(BF16)
        vp = v_refs[g][0].reshape(page_rows, SB_HEAD_DIM).astype(BF16)
        acc = acc + jnp.dot(am, vp, preferred_element_type=F32)
        carry = carry + tot[g:g + 1]
    acc_ref[...] = acc
    carry_ref[...] = carry

    @pl.when(j == pl.num_programs(1) - 1)
    def _():
        o = acc[:SB_HEADS]
        ms = jnp.mean(o * o, axis=1, keepdims=True)
        o_ref[0] = ((o * lax.rsqrt(ms + NORM_EPS)) * g_ref[...]).astype(BF16)


def _sb_decode(page_table, q, bias_lane, u8, tot8, gain, cache_k, cache_v):
    bs, n_pages = page_table.shape
    page = cache_k.shape[1]
    npg = DEC_PAGES_PER_STEP
    assert n_pages % npg == 0 and npg <= 2 * SUBLANES
    page_rows = page * SB_HEADS
    const = lambda shape: pl.BlockSpec(shape, lambda b, j, pt: (0,) * len(shape))

    def page_spec(g):
        return pl.BlockSpec((1, page, SB_HEADS, SB_HEAD_DIM),
                            lambda b, j, pt: (pt[b, n_pages - 1 - (j * npg + g)], 0, 0, 0))

    grid_spec = pltpu.PrefetchScalarGridSpec(
        num_scalar_prefetch=1,
        grid=(bs, n_pages // npg),
        in_specs=[
            pl.BlockSpec((1, 2 * SUBLANES, SB_HEAD_DIM), lambda b, j, pt: (b, 0, 0)),
            const((1, page_rows)), const((page_rows, page_rows)), const((page_rows, page_rows)),
            const((SB_HEADS, SB_HEAD_DIM)),
        ] + [page_spec(g) for g in range(npg)] + [page_spec(g) for g in range(npg)],
        out_specs=pl.BlockSpec((1, SB_HEADS, SB_HEAD_DIM), lambda b, j, pt: (b, 0, 0)),
        scratch_shapes=[pltpu.VMEM((2 * SUBLANES, SB_HEAD_DIM), F32), pltpu.VMEM((1, page_rows), F32)],
    )
    return pl.pallas_call(
        _sb_decode_kernel,
        grid_spec=grid_spec,
        out_shape=jax.ShapeDtypeStruct((bs, SB_HEADS, SB_HEAD_DIM), BF16),
        compiler_params=_cparams(("arbitrary", "arbitrary")),
        name="sb_decode",
    )(page_table, q, bias_lane, u8, tot8, gain, *([cache_k] * npg), *([cache_v] * npg))


def _conv_silu(xa, cw, cb, length):
    base = SUBLANES - (CONV_WIDTH - 1)
    out = xa[base:base + length] * cw[0:1] + cb
    for w in range(1, CONV_WIDTH):
        out = out + xa[base + w:base + w + length] * cw[w:w + 1]
    return out * jax.nn.sigmoid(out)


def _mlstm_kernel(qk_ref, v_ref, so_ref, g_ref, gt_ref, cw_ref, cb_ref, og_ref, triu_ref, tril_ref,
                  mix_ref, c_out_ref, n_out_ref, m_out_ref, c_ref, n_ref, m_ref):
    length = ML_CHUNK
    s = qk_ref.shape[1]
    n_chunks = s // length
    c_ref[...] = jnp.zeros_like(c_ref)
    n_ref[...] = jnp.zeros_like(n_ref)
    m_ref[...] = jnp.zeros_like(m_ref)
    rowi = lax.broadcasted_iota(jnp.int32, (length, length), 0)
    coli = lax.broadcasted_iota(jnp.int32, (length, length), 1)
    causal = coli <= rowi
    cw = cw_ref[...]
    cb = cb_ref[...]

    def chunk(c, carry):
        c0 = pl.multiple_of(c * length, length)
        hstart = pl.multiple_of(jnp.maximum(c0 - SUBLANES, 0), SUBLANES)
        hist = qk_ref[0, pl.ds(hstart, SUBLANES), :]
        hist = jnp.where(c > 0, hist, jnp.zeros_like(hist))
        xa = jnp.concatenate([hist, qk_ref[0, pl.ds(c0, length), :]], axis=0)
        act = _conv_silu(xa, cw, cb, length)
        gch = g_ref[0, pl.ds(c0, length), :]
        gtch = gt_ref[c]
        bcum_rows = jnp.dot(gtch, triu_ref[...], precision=HIGHEST, preferred_element_type=F32)
        bcum_cols = jnp.dot(tril_ref[...], gch, precision=HIGHEST, preferred_element_type=F32)
        vch = v_ref[0, pl.ds(c0, length), :]
        soch = so_ref[0, pl.ds(c0, length), :]
        outs = []
        for h in range(ML_HEADS):
            hs = slice(h * ML_HEAD_DIM, (h + 1) * ML_HEAD_DIM)
            qh = act[:, hs]
            kh = act[:, ML_WIDTH + h * ML_HEAD_DIM:ML_WIDTH + (h + 1) * ML_HEAD_DIM] * (ML_HEAD_DIM ** -0.5)
            vh = vch[:, hs]
            u_row = gtch[h:h + 1, :] - bcum_rows[ML_HEADS + h:ML_HEADS + h + 1, :]
            bcum_col = bcum_cols[:, ML_HEADS + h:ML_HEADS + h + 1]
            u_col = gch[:, h:h + 1] - bcum_col
            m_prev = m_ref[h]
            cm = jnp.max(jnp.where(causal, u_row, -jnp.inf), axis=1, keepdims=True)
            mm = jnp.maximum(m_prev, cm)
            d = jnp.where(causal, jnp.exp(u_row - mm), 0.0)
            inter = jnp.exp(m_prev - mm)
            m_t = bcum_col + mm
            qb = qh.astype(BF16)
            kb = kh.astype(BF16)
            w = lax.dot_general(qb, kb, (((1,), (1,)), ((), ())), preferred_element_type=F32) * d
            c_prev = c_ref[h]
            n_prev = n_ref[h]
            num = (jnp.dot(w.astype(BF16), vh, preferred_element_type=F32)
                   + inter * jnp.dot(qb, c_prev.astype(BF16), preferred_element_type=F32))
            den = (jnp.sum(w, axis=1, keepdims=True)
                   + inter * jnp.sum(qh * n_prev, axis=1, keepdims=True))
            hval = num / (jnp.maximum(jnp.abs(den), jnp.exp(-m_t)) + ML_EPS)
            m_end = mm[length - 1:length, :]
            decay = jnp.exp(u_col - m_end)
            scale = inter[length - 1:length, :]
            dk = decay * kh
            c_ref[h] = scale * c_prev + jnp.dot(dk.T.astype(BF16), vh, preferred_element_type=F32)
            n_ref[h] = scale * n_prev + jnp.sum(dk, axis=0, keepdims=True)
            m_ref[h] = m_t[length - 1:length, :]
            hms = jnp.mean(hval * hval, axis=1, keepdims=True)
            hn = (hval * lax.rsqrt(hms + NORM_EPS)) * og_ref[:, hs]
            outs.append(soch[:, hs] * hn)
        mix_ref[0, pl.ds(c0, length), :] = jnp.concatenate(outs, axis=-1).astype(BF16)
        return carry

    lax.fori_loop(0, n_chunks, chunk, 0)
    c_out_ref[0] = c_ref[...]
    n_out_ref[0] = n_ref[...]
    m_out_ref[0] = m_ref[...]


def _mlstm_prompt(qk, v, so, gates, gates_t, conv_w, conv_b, out_g, triu, tril):
    b, s, _ = qk.shape
    full = lambda shape: pl.BlockSpec(shape, lambda i: (0,) * len(shape))
    seq = lambda width: pl.BlockSpec((1, s, width), lambda i: (i, 0, 0))
    out_shape = (
        jax.ShapeDtypeStruct((b, s, ML_WIDTH), BF16),
        jax.ShapeDtypeStruct((b, ML_HEADS, ML_HEAD_DIM, ML_HEAD_DIM), F32),
        jax.ShapeDtypeStruct((b, ML_HEADS, 1, ML_HEAD_DIM), F32),
        jax.ShapeDtypeStruct((b, ML_HEADS, 1, 1), F32),
    )
    out_specs = (
        seq(ML_WIDTH),
        pl.BlockSpec((1, ML_HEADS, ML_HEAD_DIM, ML_HEAD_DIM), lambda i: (i, 0, 0, 0)),
        pl.BlockSpec((1, ML_HEADS, 1, ML_HEAD_DIM), lambda i: (i, 0, 0, 0)),
        pl.BlockSpec((1, ML_HEADS, 1, 1), lambda i: (i, 0, 0, 0)),
    )
    return pl.pallas_call(
        _mlstm_kernel,
        grid=(b,),
        in_specs=[seq(2 * ML_WIDTH), seq(ML_WIDTH), seq(ML_WIDTH), seq(LANES),
                  pl.BlockSpec((s // ML_CHUNK, 2 * GATE_LANES, ML_CHUNK), lambda i: (i, 0, 0)),
                  full((CONV_WIDTH, 2 * ML_WIDTH)), full((1, 2 * ML_WIDTH)), full((1, ML_WIDTH)),
                  full((ML_CHUNK, ML_CHUNK)), full((ML_CHUNK, ML_CHUNK))],
        out_specs=out_specs,
        out_shape=out_shape,
        scratch_shapes=[pltpu.VMEM((ML_HEADS, ML_HEAD_DIM, ML_HEAD_DIM), F32),
                        pltpu.VMEM((ML_HEADS, 1, ML_HEAD_DIM), F32),
                        pltpu.VMEM((ML_HEADS, 1, 1), F32)],
        compiler_params=_cparams(("arbitrary",)),
        name="mlstm_prompt",
    )(qk, v, so, gates, gates_t, conv_w, conv_b, out_g, triu, tril)


def _mlstm_step_kernel(qk_ref, cs_ref, v_ref, so_ref, g_ref, c_ref, n_ref, m_ref,
                       cw_ref, cb_ref, og_ref, eye_ref,
                       mix_ref, c_out_ref, n_out_ref, m_out_ref, conv_out_ref):
    new = qk_ref[0]
    state = cs_ref[0]
    cw = cw_ref[...]
    qc = state[0:1] * cw[0:1] + cb_ref[...]
    for w in range(1, CONV_WIDTH - 1):
        qc = qc + state[w:w + 1] * cw[w:w + 1]
    qc = qc + new * cw[CONV_WIDTH - 1:CONV_WIDTH]
    act = qc * jax.nn.sigmoid(qc)
    conv_out_ref[0] = jnp.concatenate([state[1:], new], axis=0)
    gates = g_ref[0]
    eye = eye_ref[...]
    outs = []
    for h in range(ML_HEADS):
        hs = slice(h * ML_HEAD_DIM, (h + 1) * ML_HEAD_DIM)
        q_row = act[:, hs]
        k_row = act[:, ML_WIDTH + h * ML_HEAD_DIM:ML_WIDTH + (h + 1) * ML_HEAD_DIM] * (ML_HEAD_DIM ** -0.5)
        v_row = v_ref[0][:, hs].astype(F32)
        q_col = jnp.sum(eye * q_row, axis=1, keepdims=True)
        k_col = jnp.sum(eye * k_row, axis=1, keepdims=True)
        ig = gates[:, h:h + 1]
        lf = gates[:, ML_HEADS + h:ML_HEADS + h + 1]
        m_prev = m_ref[0, h]
        c_prev = c_ref[0, h]
        n_prev = n_ref[0, h]
        m_inter = lf + m_prev
        m_t = jnp.maximum(m_inter, ig)
        d = jnp.exp(ig - m_t)
        inter = jnp.exp(m_inter - m_t)
        w = jnp.sum(q_row * k_row, axis=1, keepdims=True) * d
        num = w * v_row + inter * jnp.sum(q_col * c_prev, axis=0, keepdims=True)
        den = w + inter * jnp.sum(q_row * n_prev, axis=1, keepdims=True)
        hval = num / (jnp.maximum(jnp.abs(den), jnp.exp(-m_t)) + ML_EPS)
        c_out_ref[0, h] = inter * c_prev + d * (k_col * v_row)
        n_out_ref[0, h] = inter * n_prev + d * k_row
        m_out_ref[0, h] = m_t
        hms = jnp.mean(hval * hval, axis=1, keepdims=True)
        hn = (hval * lax.rsqrt(hms + NORM_EPS)) * og_ref[:, hs]
        outs.append(so_ref[0][:, hs] * hn)
    mix_ref[0] = jnp.concatenate(outs, axis=-1).astype(BF16)


def _mlstm_sample(qk, conv_state, v, so, gates, c, n, m, conv_w, conv_b, out_g, eye):
    bs = qk.shape[0]
    full = lambda shape: pl.BlockSpec(shape, lambda i: (0,) * len(shape))
    row3 = lambda r, width: pl.BlockSpec((1, r, width), lambda i: (i, 0, 0))
    st4 = lambda a, b2: pl.BlockSpec((1, ML_HEADS, a, b2), lambda i: (i, 0, 0, 0))
    out_shape = (
        jax.ShapeDtypeStruct((bs, 1, ML_WIDTH), BF16),
        jax.ShapeDtypeStruct((bs, ML_HEADS, ML_HEAD_DIM, ML_HEAD_DIM), F32),
        jax.ShapeDtypeStruct((bs, ML_HEADS, 1, ML_HEAD_DIM), F32),
        jax.ShapeDtypeStruct((bs, ML_HEADS, 1, 1), F32),
        jax.ShapeDtypeStruct((bs, CONV_WIDTH - 1, 2 * ML_WIDTH), F32),
    )
    return pl.pallas_call(
        _mlstm_step_kernel,
        grid=(bs,),
        in_specs=[row3(1, 2 * ML_WIDTH), row3(CONV_WIDTH - 1, 2 * ML_WIDTH), row3(1, ML_WIDTH),
                  row3(1, ML_WIDTH), row3(1, LANES),
                  st4(ML_HEAD_DIM, ML_HEAD_DIM), st4(1, ML_HEAD_DIM), st4(1, 1),
                  full((CONV_WIDTH, 2 * ML_WIDTH)), full((1, 2 * ML_WIDTH)), full((1, ML_WIDTH)),
                  full((ML_HEAD_DIM, ML_HEAD_DIM))],
        out_specs=(row3(1, ML_WIDTH), st4(ML_HEAD_DIM, ML_HEAD_DIM), st4(1, ML_HEAD_DIM), st4(1, 1),
                   row3(CONV_WIDTH - 1, 2 * ML_WIDTH)),
        out_shape=out_shape,
        compiler_params=_cparams(("arbitrary",)),
        name="mlstm_sample",
    )(qk, conv_state, v, so, gates, c, n, m, conv_w, conv_b, out_g, eye)


def _mix_router_body(x, sb, ml, wo_ref, gf_ref, wrh_ref, wrl_ref, br_ref, tri_ref, cnt_ref,
                     h_ref, xn_ref, idx_ref, gate_ref, rank_ref):
    rows = x.shape[0]
    h = (x + jnp.dot(sb, wo_ref[:SB_WIDTH, :], preferred_element_type=F32)
         + jnp.dot(ml, wo_ref[SB_WIDTH:, :], preferred_element_type=F32))
    h_ref[0:rows, :] = h
    ms = jnp.mean(h * h, axis=-1, keepdims=True)
    xn = (h * lax.rsqrt(ms + NORM_EPS)) * gf_ref[...]
    xh, xl = _split_bf16(xn)
    bits = pltpu.bitcast(xh.astype(F32), jnp.uint32)
    half = bits.shape[1] // 2
    xn_ref[0:rows, :] = (bits[:, :half] >> 16) | (bits[:, half:] & jnp.uint32(0xFFFF0000))
    logits = (jnp.dot(xh, wrh_ref[...], preferred_element_type=F32)
              + jnp.dot(xl, wrh_ref[...], preferred_element_type=F32)
              + jnp.dot(xh, wrl_ref[...], preferred_element_type=F32)) + br_ref[...]
    lane = lax.broadcasted_iota(jnp.int32, logits.shape, 1).astype(F32)
    neg = -jnp.inf
    logits = jnp.where(lane < N_EXPERTS, logits, neg)
    tops, idxs = [], []
    onehot = jnp.zeros(logits.shape, F32)
    for _ in range(TOP_K):
        mx = jnp.max(logits, axis=1, keepdims=True)
        ix = jnp.min(jnp.where(logits == mx, lane, float(LANES)), axis=1, keepdims=True)
        hit = lane == ix
        onehot = jnp.where(hit, 1.0, onehot)
        logits = jnp.where(hit, neg, logits)
        tops.append(mx)
        idxs.append(ix)
    es = [jnp.exp(t - tops[0]) for t in tops]
    denom = es[0] + es[1] + es[2] + es[3]
    cum = (jnp.dot(tri_ref[0:rows, 0:rows], onehot.astype(BF16), preferred_element_type=F32)
           + cnt_ref[...])
    cnt_ref[...] = cnt_ref[...] + jnp.sum(onehot, axis=0, keepdims=True)
    idx_out = jnp.zeros(logits.shape, F32)
    gate_out = jnp.zeros(logits.shape, F32)
    rank_out = jnp.zeros(logits.shape, F32)
    for k in range(TOP_K):
        rk = jnp.sum(jnp.where(lane == idxs[k], cum, 0.0), axis=1, keepdims=True)
        idx_out = jnp.where(lane == k, idxs[k], idx_out)
        gate_out = jnp.where(lane == k, es[k] / denom, gate_out)
        rank_out = jnp.where(lane == k, rk, rank_out)
    idx_ref[0:rows, :] = idx_out[:, :TOP_K].astype(jnp.int32)
    gate_ref[0:rows, :] = gate_out[:, :TOP_K]
    rank_ref[0:rows, :] = rank_out[:, :TOP_K].astype(jnp.int32)


def _mix_router_kernel(xp_ref, sbp_ref, mlp_ref, xs_ref, sbs_ref, mls_ref,
                       wo_ref, gf_ref, wrh_ref, wrl_ref, br_ref, tri_ref,
                       h_ref, xn_ref, idx_ref, gate_ref, rank_ref, cnt_out_ref, cnt_ref):
    i = pl.program_id(0)
    last = pl.num_programs(0) - 1

    @pl.when(i == 0)
    def _():
        cnt_ref[...] = jnp.zeros_like(cnt_ref)

    common = (wo_ref, gf_ref, wrh_ref, wrl_ref, br_ref, tri_ref, cnt_ref,
              h_ref, xn_ref, idx_ref, gate_ref, rank_ref)

    @pl.when(i < last)
    def _():
        _mix_router_body(xp_ref[...], sbp_ref[...], mlp_ref[...], *common)

    @pl.when(i == last)
    def _():
        _mix_router_body(xs_ref[...], sbs_ref[...], mls_ref[...], *common)
        cnt_out_ref[...] = cnt_ref[...]


def _mix_router(xp, sbp, mlp, xs, sbs, mls, w_out, g_ffn, wr_hi, wr_lo, b_router, tri):
    tp, d = xp.shape
    ts = xs.shape[0]
    rows = ROUTER_ROWS
    assert tp % rows == 0 and ts <= rows
    nblk = tp // rows
    t_all = tp + ts
    full = lambda shape: pl.BlockSpec(shape, lambda i: (0,) * len(shape))
    pblk = lambda width: pl.BlockSpec((rows, width), lambda i: (jnp.minimum(i, nblk - 1), 0))
    oblk = lambda width: pl.BlockSpec((rows, width), lambda i: (i, 0))
    out_shape = (
        jax.ShapeDtypeStruct((t_all, d), F32),
        jax.ShapeDtypeStruct((t_all, d // 2), jnp.uint32),
        jax.ShapeDtypeStruct((t_all, TOP_K), jnp.int32),
        jax.ShapeDtypeStruct((t_all, TOP_K), F32),
        jax.ShapeDtypeStruct((t_all, TOP_K), jnp.int32),
        jax.ShapeDtypeStruct((1, LANES), F32),
    )
    return pl.pallas_call(
        _mix_router_kernel,
        grid=(nblk + 1,),
        in_specs=[pblk(d), pblk(SB_WIDTH), pblk(ML_WIDTH),
                  full((ts, d)), full((ts, SB_WIDTH)), full((ts, ML_WIDTH)),
                  full(w_out.shape), full((1, d)), full((d, LANES)), full((d, LANES)),
                  full((1, LANES)), full((rows, rows))],
        out_specs=(oblk(d), oblk(d // 2), oblk(TOP_K), oblk(TOP_K), oblk(TOP_K), full((1, LANES))),
        out_shape=out_shape,
        scratch_shapes=[pltpu.VMEM((1, LANES), F32)],
        compiler_params=_cparams(("arbitrary",)),
        name="mix_router",
    )(xp, sbp, mlp, xs, sbs, mls, w_out, g_ffn, wr_hi, wr_lo, b_router, tri)


def _for_tokens(n_tok, issue):
    def body(t, c):
        issue(t)
        return c

    lax.fori_loop(0, n_tok, body, 0, unroll=ROW_DMA_UNROLL)


def _block_cases(i, n_tokens, rows, run):
    n_full, rem = divmod(n_tokens, rows)
    if n_full:
        pl.when(i < n_full)(lambda: run(rows))
    if rem:
        pl.when(i == n_full)(lambda: run(rem))


def _dispatch_kernel(dest_ref, xn_ref, buf_in_ref, buf_ref, sem, *, n_tokens):
    del buf_in_ref

    def run(n_tok):
        def issue(t):
            src = xn_ref.at[pl.ds(t, 1)]
            for k in range(TOP_K):
                pltpu.make_async_copy(src, buf_ref.at[pl.ds(dest_ref[t * TOP_K + k], 1)], sem).start()

        _for_tokens(n_tok, issue)
        for _ in range(TOP_K):
            pltpu.make_async_copy(xn_ref.at[pl.ds(0, n_tok)], buf_ref.at[pl.ds(0, n_tok)], sem).wait()

    _block_cases(pl.program_id(0), n_tokens, xn_ref.shape[0], run)


def _dispatch(dest_flat, xn, buf):
    t, d = xn.shape
    rows = DISPATCH_ROWS
    return pl.pallas_call(
        functools.partial(_dispatch_kernel, n_tokens=t),
        grid=(pl.cdiv(t, rows),),
        in_specs=[pl.BlockSpec((rows * TOP_K,), lambda i: (i,), memory_space=pltpu.SMEM),
                  pl.BlockSpec((rows, d), lambda i: (i, 0)),
                  pl.BlockSpec(memory_space=pl.ANY)],
        out_specs=pl.BlockSpec(memory_space=pl.ANY),
        out_shape=jax.ShapeDtypeStruct(buf.shape, buf.dtype),
        scratch_shapes=[pltpu.SemaphoreType.DMA(())],
        input_output_aliases={2: 0},
        compiler_params=_cparams(("arbitrary",)),
        name="dispatch",
    )(dest_flat, xn, buf)


def _experts_kernel(blk_ref, exp_ref, used_ref, x_ref, wu_ref, bu_ref, wd_ref, bd_ref, y_ref):
    i = pl.program_id(0)

    @pl.when(i < used_ref[0])
    def _():
        de = wd_ref.shape[1]
        words = x_ref[...]
        half = words.shape[1]
        x_lo = pltpu.bitcast(words << 16, F32).astype(BF16)
        x_hi = pltpu.bitcast(words & jnp.uint32(0xFFFF0000), F32).astype(BF16)
        hmid = (jnp.dot(x_lo, wu_ref[0, :half, :], preferred_element_type=F32)
                + jnp.dot(x_hi, wu_ref[0, half:, :], preferred_element_type=F32)) + bu_ref[0]
        g = jnp.minimum(hmid[:, :de], SWIGLU_LIMIT)
        lin = jnp.clip(hmid[:, de:], -SWIGLU_LIMIT, SWIGLU_LIMIT)
        act = (lin + 1.0) * g * jax.nn.sigmoid(SWIGLU_ALPHA * g)
        y_ref[...] = jnp.dot(act.astype(BF16), wd_ref[0], preferred_element_type=F32) + bd_ref[0]


def _experts(blk_row, blk_exp, n_used, xbuf, w_up, b_up, w_down, b_down):
    cap = xbuf.shape[0]
    d = w_up.shape[1]
    rows = EXPERT_ROWS
    n_blocks = blk_row.shape[0]
    de2 = w_up.shape[2]
    grid_spec = pltpu.PrefetchScalarGridSpec(
        num_scalar_prefetch=3,
        grid=(n_blocks,),
        in_specs=[
            pl.BlockSpec((rows, d // 2), lambda i, br, be, nu: (br[i], 0)),
            pl.BlockSpec((1, d, de2), lambda i, br, be, nu: (be[i], 0, 0)),
            pl.BlockSpec((1, 1, de2), lambda i, br, be, nu: (be[i], 0, 0)),
            pl.BlockSpec((1, de2 // 2, d), lambda i, br, be, nu: (be[i], 0, 0)),
            pl.BlockSpec((1, 1, d), lambda i, br, be, nu: (be[i], 0, 0)),
        ],
        out_specs=pl.BlockSpec((rows, d), lambda i, br, be, nu: (br[i], 0)),
    )
    return pl.pallas_call(
        _experts_kernel,
        grid_spec=grid_spec,
        out_shape=jax.ShapeDtypeStruct((cap, d), F32),
        compiler_params=_cparams(("arbitrary",)),
        name="experts",
    )(blk_row, blk_exp, n_used, xbuf, w_up, b_up, w_down, b_down)


def _combine_kernel(dest_ref, dest_next_ref, h_ref, gate_ref, y_ref, op_ref, os_ref, rows_ref, sem,
                    *, n_prompt, n_tokens):
    i = pl.program_id(0)
    rows = h_ref.shape[0]
    slot = i & 1

    def fetch(dref, s):
        def run(n_tok):
            def issue(t):
                for k in range(TOP_K):
                    pltpu.make_async_copy(y_ref.at[pl.ds(dref[t * TOP_K + k], 1)],
                                          rows_ref.at[s, k, pl.ds(t, 1)], sem.at[s]).start()

            _for_tokens(n_tok, issue)
        return run

    def drain(n_tok):
        for k in range(TOP_K):
            pltpu.make_async_copy(y_ref.at[pl.ds(0, n_tok)], rows_ref.at[slot, k, pl.ds(0, n_tok)],
                                  sem.at[slot]).wait()

    def combine(n_tok):
        drain(n_tok)
        gates = gate_ref[0:n_tok, :]
        out = h_ref[0:n_tok, :]
        for k in range(TOP_K):
            out = out + gates[:, k:k + 1] * rows_ref[slot, k, 0:n_tok, :]
        return out

    @pl.when(i == 0)
    def _():
        fetch(dest_ref, 0)(rows)

    _block_cases(i + 1, n_tokens, rows, fetch(dest_next_ref, 1 - slot))

    n_pblk = n_prompt // rows

    @pl.when(i < n_pblk)
    def _():
        op_ref[...] = combine(rows)

    @pl.when(i == n_pblk)
    def _():
        os_ref[...] = combine(n_tokens - n_prompt)


def _combine(dest_flat, h, gates, y, n_prompt):
    t, d = h.shape
    rows = COMBINE_ROWS
    n_sample = t - n_prompt
    assert n_prompt % rows == 0 and 0 < n_sample <= rows
    n_steps = n_prompt // rows + 1
    return pl.pallas_call(
        functools.partial(_combine_kernel, n_prompt=n_prompt, n_tokens=t),
        grid=(n_steps,),
        in_specs=[pl.BlockSpec((rows * TOP_K,), lambda i: (i,), memory_space=pltpu.SMEM),
                  pl.BlockSpec((rows * TOP_K,), lambda i: (jnp.minimum(i + 1, n_steps - 1),),
                               memory_space=pltpu.SMEM),
                  pl.BlockSpec((rows, d), lambda i: (i, 0)),
                  pl.BlockSpec((rows, TOP_K), lambda i: (i, 0)),
                  pl.BlockSpec(memory_space=pl.ANY)],
        out_specs=(pl.BlockSpec((rows, d), lambda i: (jnp.minimum(i, n_steps - 2), 0)),
                   pl.BlockSpec((n_sample, d), lambda i: (0, 0))),
        out_shape=(jax.ShapeDtypeStruct((n_prompt, d), F32), jax.ShapeDtypeStruct((n_sample, d), F32)),
        scratch_shapes=[pltpu.VMEM((2, TOP_K, rows, d), F32), pltpu.SemaphoreType.DMA((2,))],
        compiler_params=_cparams(("arbitrary",)),
        name="combine",
    )(dest_flat, dest_flat, h, gates, y)


def _layer(x_prompt, x_sample, cache_k, cache_v, page_table, st_c, st_n, st_m, st_conv,
           norm_mix_g, w_in, b_gates, q_norm_g, k_norm_g, sb_bias, conv_w, conv_b,
           sb_out_g, ml_out_g, w_out, norm_ffn_g, w_router, b_router, w_up, b_up, w_down, b_down):
    bp, sp, d = x_prompt.shape
    bs, ss, _ = x_sample.shape
    assert ss == 1, "decode kernels are written for one new token per sequence"
    tp, ts = bp * sp, bs * ss
    n_main = 3 * SB_WIDTH + 4 * ML_WIDTH

    w_main = w_in[:, :n_main].astype(BF16)
    wg = w_in[:, n_main:]
    w_gate = jnp.pad(wg, ((0, 0), (0, LANES - GATE_LANES))).astype(BF16)
    w_gate_t = jnp.pad(wg.T, ((0, GATE_LANES), (0, 0))).astype(BF16)
    b_gate = jnp.pad(b_gates, (0, LANES - GATE_LANES)).reshape(1, LANES)
    b_gate_t = jnp.pad(b_gates, (0, GATE_LANES)).reshape(2 * GATE_LANES, 1)
    slab = 2 * LANES
    bd = jnp.kron(jnp.eye(slab // SB_HEAD_DIM, dtype=F32),
                  jnp.full((SB_HEAD_DIM, SB_HEAD_DIM), 1.0 / SB_HEAD_DIM, F32)).astype(BF16)
    qk_gain = jnp.concatenate([jnp.tile(q_norm_g, SB_HEADS), jnp.tile(k_norm_g, SB_HEADS)]).reshape(1, -1)
    norm_g = norm_mix_g.reshape(1, d)

    def suffix_ones(n):
        r = jnp.arange(n)
        return (r[:, None] >= r[None, :])

    proj = functools.partial(_proj, norm_g=norm_g, w_main=w_main, w_gate=w_gate, w_gate_t=w_gate_t,
                             bd=bd, qk_gain=qk_gain, b_gate=b_gate, b_gate_t=b_gate_t)
    (q_p, k_p, kb_p, v_p, vb_p, mqk_p, mv_p, so_p, gate_p, gatet_p) = proj(
        x_prompt.reshape(tp, d), rows=PROJ_ROWS)
    (q_s, k_s, _, v_s, _, mqk_s, mv_s, so_s, gate_s, _) = proj(x_sample.reshape(ts, d), rows=ts)

    u_blk = suffix_ones(SB_BLOCK).astype(BF16)
    sb_gain = sb_out_g.reshape(SB_WIDTH // LANES, 1, LANES)
    mix_sb_p = _sb_prompt(sb_bias, q_p.reshape(bp, sp, SB_WIDTH), kb_p.reshape(bp, sp, SB_WIDTH),
                          vb_p.reshape(bp, sp, SB_WIDTH), u_blk, sb_gain)
    tril = suffix_ones(ML_CHUNK).astype(F32)
    triu = tril.T
    ml_gain = ml_out_g.reshape(1, ML_WIDTH)
    conv_b2 = conv_b.reshape(1, 2 * ML_WIDTH)
    gatet_chunks = gatet_p.reshape(2 * GATE_LANES, tp // ML_CHUNK, ML_CHUNK).transpose(1, 0, 2)
    mix_ml_p, c_p, n_p, m_p = _mlstm_prompt(
        mqk_p.reshape(bp, sp, 2 * ML_WIDTH), mv_p.reshape(bp, sp, ML_WIDTH),
        so_p.reshape(bp, sp, ML_WIDTH), gate_p.reshape(bp, sp, LANES), gatet_chunks,
        conv_w, conv_b2, ml_gain, triu, tril)
    conv_p = mqk_p.reshape(bp, sp, 2 * ML_WIDTH)[:, sp - (CONV_WIDTH - 1):]

    page = cache_k.shape[1]
    same_head = jnp.eye(SB_HEADS, dtype=jnp.bool_)
    tot8 = jnp.tile(same_head, (page, page))
    u8 = jnp.logical_and(tot8, jnp.repeat(jnp.repeat(suffix_ones(page), SB_HEADS, 0), SB_HEADS, 1))
    q_dec = jnp.pad(q_s.reshape(bs, SB_HEADS, SB_HEAD_DIM), ((0, 0), (0, 2 * SUBLANES - SB_HEADS), (0, 0)))
    mix_sb_s = _sb_decode(page_table, q_dec, jnp.tile(sb_bias, page).reshape(1, -1),
                          u8.astype(BF16), tot8.astype(BF16), sb_out_g, cache_k, cache_v)
    eye = jnp.eye(ML_HEAD_DIM, dtype=F32)
    mix_ml_s, c_s, n_s, m_s, conv_s = _mlstm_sample(
        mqk_s.reshape(bs, 1, 2 * ML_WIDTH), st_conv, mv_s.reshape(bs, 1, ML_WIDTH),
        so_s.reshape(bs, 1, ML_WIDTH), gate_s.reshape(bs, 1, LANES),
        st_c, st_n.reshape(bs, ML_HEADS, 1, ML_HEAD_DIM), st_m.reshape(bs, ML_HEADS, 1, 1),
        conv_w, conv_b2, ml_gain, eye)

    wr = jnp.pad(w_router, ((0, 0), (0, LANES - N_EXPERTS)))
    wr_hi = wr.astype(BF16)
    wr_lo = (wr - wr_hi.astype(F32)).astype(BF16)
    br = jnp.pad(b_router, (0, LANES - N_EXPERTS)).reshape(1, LANES)
    r = jnp.arange(ROUTER_ROWS)
    tri_strict = (r[None, :] < r[:, None]).astype(BF16)
    h_all, xn_all, idx, gates, rank, counts = _mix_router(
        x_prompt.reshape(tp, d), mix_sb_p.reshape(tp, SB_WIDTH), mix_ml_p.reshape(tp, ML_WIDTH),
        x_sample.reshape(ts, d), mix_sb_s.reshape(ts, SB_WIDTH), mix_ml_s.reshape(ts, ML_WIDTH),
        w_out.astype(BF16), norm_ffn_g.reshape(1, d), wr_hi, wr_lo, br, tri_strict)

    t_all = tp + ts
    rows = EXPERT_ROWS
    cnt = counts[0, :N_EXPERTS].astype(jnp.int32)
    nblk_e = (cnt + rows - 1) // rows
    blk_end = jnp.cumsum(nblk_e)
    blk_start = blk_end - nblk_e
    n_blocks = -(-t_all * TOP_K // rows) + N_EXPERTS
    dest = (blk_start[idx] * rows + rank).reshape(-1)
    step = max(DISPATCH_ROWS, COMBINE_ROWS) * TOP_K
    dest = jnp.pad(dest, (0, -dest.shape[0] % step))
    bid = jnp.arange(n_blocks, dtype=jnp.int32)
    n_used = blk_end[-1]
    bclamp = jnp.minimum(bid, n_used - 1)
    blk_exp = jnp.minimum(jnp.sum(bclamp[:, None] >= blk_end[None, :], axis=1), N_EXPERTS - 1).astype(jnp.int32)
    cap = n_blocks * rows

    xbuf = _dispatch(dest, xn_all, jnp.zeros((cap, d // 2), jnp.uint32))
    y = _experts(bclamp, blk_exp, n_used.reshape(1), xbuf,
                 w_up.astype(BF16), b_up.reshape(N_EXPERTS, 1, -1),
                 w_down.astype(BF16), b_down.reshape(N_EXPERTS, 1, -1))
    out_p, out_s = _combine(dest, h_all, gates, y, tp)

    y_prompt = out_p.reshape(bp, sp, d)
    y_sample = out_s.reshape(bs, ss, d)
    return (y_prompt, y_sample,
            k_p.reshape(bp, sp, SB_HEADS, SB_HEAD_DIM), v_p.reshape(bp, sp, SB_HEADS, SB_HEAD_DIM),
            k_s.reshape(bs, ss, SB_HEADS, SB_HEAD_DIM), v_s.reshape(bs, ss, SB_HEADS, SB_HEAD_DIM),
            c_p, n_p.reshape(bp, ML_HEADS, ML_HEAD_DIM), m_p.reshape(bp, ML_HEADS), conv_p,
            c_s, n_s.reshape(bs, ML_HEADS, ML_HEAD_DIM), m_s.reshape(bs, ML_HEADS), conv_s)


def kernel(x_prompt, x_sample, cache_sb_k, cache_sb_v, page_table, state_mlstm_C, state_mlstm_n, state_mlstm_m, state_mlstm_conv, norm_mix_g, w_in, b_gates, q_norm_g, k_norm_g, sb_bias, conv_w, conv_b, sb_out_g, ml_out_g, w_out, norm_ffn_g, w_router, b_router, w_up, b_up, w_down, b_down):
    depth = w_in.shape[0]
    assert depth == 1, "single-layer step"
    outs = _layer(x_prompt, x_sample, cache_sb_k[0], cache_sb_v[0], page_table,
                  state_mlstm_C[0], state_mlstm_n[0], state_mlstm_m[0], state_mlstm_conv[0],
                  norm_mix_g[0], w_in[0], b_gates[0], q_norm_g[0], k_norm_g[0], sb_bias[0],
                  conv_w[0], conv_b[0], sb_out_g[0], ml_out_g[0], w_out[0], norm_ffn_g[0],
                  w_router[0], b_router[0], w_up[0], b_up[0], w_down[0], b_down[0])
    return tuple(o[None] if i >= 2 else o for i, o in enumerate(outs))
```

```python
import functools

import jax
import jax.numpy as jnp
from jax import lax
from jax.experimental import pallas as pl
from jax.experimental.pallas import tpu as pltpu

F32 = jnp.float32
BF16 = jnp.bfloat16
HIGHEST = lax.Precision.HIGHEST

SB_HEADS = 8
SB_HEAD_DIM = 64
SB_WIDTH = SB_HEADS * SB_HEAD_DIM
SB_SCALE = SB_HEAD_DIM ** -0.5
ML_HEADS = 4
ML_HEAD_DIM = 128
ML_WIDTH = ML_HEADS * ML_HEAD_DIM
ML_EPS = 1e-6
CONV_WIDTH = 4
N_EXPERTS = 32
TOP_K = 4
SWIGLU_LIMIT = 7.0
SWIGLU_ALPHA = 1.702
NORM_EPS = 1e-6
LOG2E = 1.4426950408889634

LANES = 128
SUBLANES = 8
VMEM_LIMIT_BYTES = 56 * 1024 * 1024

PROJ_ROWS = 256
SB_BLOCK = 256
ML_CHUNK = 128
DEC_PAGES_PER_STEP = 8
ROUTER_ROWS = 512
EXPERT_ROWS = 512
DISPATCH_ROWS = 512
COMBINE_ROWS = 256
ROW_DMA_UNROLL = 8
GATE_LANES = 2 * ML_HEADS


def _cparams(sem, vmem=VMEM_LIMIT_BYTES):
    return pltpu.CompilerParams(dimension_semantics=sem, vmem_limit_bytes=vmem)


def _softplus(z):
    return jnp.maximum(z, 0.0) + jnp.log(1.0 + jnp.exp2(jnp.abs(z) * (-LOG2E)))


def _exp(x):
    return jnp.exp2(x * LOG2E)


def _split_bf16(x):
    hi = x.astype(BF16)
    lo = (x - hi.astype(F32)).astype(BF16)
    return hi, lo


def _proj_kernel(x_ref, g_ref, w_ref, wg_ref, wgt_ref, bd_ref, qkg_ref, bg_ref, bgt_ref,
                 q_ref, k_ref, kb_ref, v_ref, vb_ref, mqk_ref, mv_ref, so_ref, gate_ref, gatet_ref):
    x = x_ref[...]
    ms = jnp.mean(x * x, axis=-1, keepdims=True)
    xn = (x * lax.rsqrt(ms + NORM_EPS)) * g_ref[...]
    xb = xn.astype(BF16)

    def proj(lo, hi):
        return jnp.dot(xb, w_ref[:, lo:hi], preferred_element_type=F32)

    o1, o2, o3 = SB_WIDTH, 2 * SB_WIDTH, 3 * SB_WIDTH
    pqk = proj(0, o2)
    sq_hi, sq_lo = _split_bf16(pqk * pqk)
    bd = bd_ref[...]
    slab = bd.shape[0]
    parts = []
    for s in range(o2 // slab):
        sl = slice(s * slab, (s + 1) * slab)
        parts.append(jnp.dot(sq_hi[:, sl], bd, preferred_element_type=F32)
                     + jnp.dot(sq_lo[:, sl], bd, preferred_element_type=F32))
    msq = jnp.concatenate(parts, axis=-1)
    qkn = (pqk * lax.rsqrt(msq + NORM_EPS)) * qkg_ref[...]
    def store_cache_rows(ref, val):
        if len(ref.shape) == 4:
            ref[0] = val.T.reshape(SB_HEADS, SB_HEAD_DIM, val.shape[0])
        else:
            ref[...] = val

    q_ref[...] = (qkn[:, :o1] * SB_SCALE).astype(BF16)
    kn = qkn[:, o1:]
    store_cache_rows(k_ref, kn)
    kb_ref[...] = kn.astype(BF16)
    pv = proj(o2, o3)
    store_cache_rows(v_ref, pv)
    vb_ref[...] = pv.astype(BF16)
    mqk_ref[...] = proj(o3, o3 + 2 * ML_WIDTH)
    mv_ref[...] = proj(o3 + 2 * ML_WIDTH, o3 + 3 * ML_WIDTH).astype(BF16)
    so_ref[...] = jax.nn.sigmoid(proj(o3 + 3 * ML_WIDTH, o3 + 4 * ML_WIDTH))

    def gate_act(g, is_ig):
        lf = jnp.minimum(g, 0.0) - jnp.log(1.0 + jnp.exp(-jnp.abs(g)))
        return jnp.where(is_ig, g, lf)

    g = jnp.dot(xb, wg_ref[...], preferred_element_type=F32) + bg_ref[...]
    lane = lax.broadcasted_iota(jnp.int32, g.shape, 1)
    gate_ref[...] = gate_act(g, lane < ML_HEADS)
    gt = lax.dot_general(wgt_ref[...], xb, (((1,), (1,)), ((), ())),
                         preferred_element_type=F32) + bgt_ref[...]
    row = lax.broadcasted_iota(jnp.int32, gt.shape, 0)
    gatet_ref[...] = gate_act(gt, row < ML_HEADS)


def _proj(x2d, norm_g, w_main, w_gate, w_gate_t, bd, qk_gain, b_gate, b_gate_t, rows, seq_len=None):
    t, d = x2d.shape
    n_main = w_main.shape[1]
    full = lambda shape: pl.BlockSpec(shape, lambda i: (0,) * len(shape))
    rowblk = lambda width: pl.BlockSpec((rows, width), lambda i: (i, 0))
    if seq_len is None:
        kv_shape = jax.ShapeDtypeStruct((t, SB_WIDTH), F32)
        kv_spec = rowblk(SB_WIDTH)
    else:
        assert seq_len % rows == 0 and t % seq_len == 0 and rows % LANES == 0
        per_seq = seq_len // rows
        kv_shape = jax.ShapeDtypeStruct((t // seq_len, SB_HEADS, SB_HEAD_DIM, seq_len), F32)
        kv_spec = pl.BlockSpec((1, SB_HEADS, SB_HEAD_DIM, rows),
                               lambda i: (i // per_seq, 0, 0, i % per_seq))
    out_shape = (
        jax.ShapeDtypeStruct((t, SB_WIDTH), BF16),
        kv_shape,
        jax.ShapeDtypeStruct((t, SB_WIDTH), BF16),
        kv_shape,
        jax.ShapeDtypeStruct((t, SB_WIDTH), BF16),
        jax.ShapeDtypeStruct((t, 2 * ML_WIDTH), F32),
        jax.ShapeDtypeStruct((t, ML_WIDTH), BF16),
        jax.ShapeDtypeStruct((t, ML_WIDTH), F32),
        jax.ShapeDtypeStruct((t, LANES), F32),
        jax.ShapeDtypeStruct((2 * GATE_LANES, t), F32),
    )
    out_specs = (
        rowblk(SB_WIDTH), kv_spec, rowblk(SB_WIDTH), kv_spec, rowblk(SB_WIDTH),
        rowblk(2 * ML_WIDTH), rowblk(ML_WIDTH), rowblk(ML_WIDTH), rowblk(LANES),
        pl.BlockSpec((2 * GATE_LANES, rows), lambda i: (0, i)),
    )
    return pl.pallas_call(
        _proj_kernel,
        grid=(t // rows,),
        in_specs=[rowblk(d), full((1, d)), full((d, n_main)), full((d, LANES)),
                  full((2 * GATE_LANES, d)), full(bd.shape), full((1, 2 * SB_WIDTH)),
                  full((1, LANES)), full((2 * GATE_LANES, 1))],
        out_specs=out_specs,
        out_shape=out_shape,
        compiler_params=_cparams(("arbitrary",)),
        name="proj",
    )(x2d, norm_g, w_main, w_gate, w_gate_t, bd, qk_gain, b_gate, b_gate_t)


def _head_norm_pair(o, gain):
    lane = lax.broadcasted_iota(jnp.int32, o.shape, 1)
    first = lane < SB_HEAD_DIM
    sq = o * o
    ms0 = jnp.sum(jnp.where(first, sq, 0.0), axis=1, keepdims=True)
    ms1 = jnp.sum(jnp.where(first, 0.0, sq), axis=1, keepdims=True)
    ms = jnp.where(first, ms0, ms1) * (1.0 / SB_HEAD_DIM)
    return (o * lax.rsqrt(ms + NORM_EPS)) * gain


def _sb_prompt_kernel(bias_ref, q_ref, k_ref, v_ref, u_ref, g_ref, o_ref, acc_ref, carry_ref):
    hp = pl.program_id(1)
    i = pl.program_id(2)
    blk = q_ref.shape[1]
    q = q_ref[0].astype(F32)
    lane = lax.broadcasted_iota(jnp.int32, q.shape, 1)
    first = lane < SB_HEAD_DIM
    qs = jnp.concatenate([jnp.where(first, q, 0.0), jnp.where(first, 0.0, q)], axis=0).astype(BF16)
    rowc = lax.broadcasted_iota(jnp.int32, (2 * blk, 1), 0)
    bias = jnp.where(rowc < blk, bias_ref[2 * hp], bias_ref[2 * hp + 1])
    u = u_ref[...]
    rowi = lax.broadcasted_iota(jnp.int32, (2 * blk, blk), 0)
    coli = lax.broadcasted_iota(jnp.int32, (2 * blk, blk), 1)
    causal = coli < jnp.where(rowi >= blk, rowi - blk, rowi)

    acc_ref[...] = jnp.zeros_like(acc_ref)
    carry_ref[...] = jnp.zeros_like(carry_ref)

    def step(j, diag):
        start = pl.multiple_of(j * blk, blk)
        kblk = k_ref[0, pl.ds(start, blk), :]
        vblk = v_ref[0, pl.ds(start, blk), :]
        z = lax.dot_general(qs, kblk, (((1,), (1,)), ((), ())), preferred_element_type=F32) + bias
        sp = _softplus(z)
        if diag:
            sp = jnp.where(causal, sp, 0.0)
        cs = jnp.dot(sp.astype(BF16), u, preferred_element_type=F32)
        a = _exp((z - cs) - carry_ref[...])
        if diag:
            a = jnp.where(causal, a, 0.0)
        acc_ref[...] += jnp.dot(a.astype(BF16), vblk, preferred_element_type=F32)
        carry_ref[...] += cs[:, 0:1]

    step(i, True)

    def pair(jj, c):
        step(i - 1 - 2 * jj, False)
        step(i - 2 - 2 * jj, False)
        return c

    lax.fori_loop(0, lax.shift_right_logical(i, 1), pair, 0)

    @pl.when((i & 1) == 1)
    def _():
        step(0, False)

    o = jnp.where(first, acc_ref[0:blk, :], acc_ref[blk:2 * blk, :])
    o_ref[0] = _head_norm_pair(o, g_ref[0]).astype(BF16)


def _sb_prompt(bias, q, k, v, u, gain):
    b, s, _ = q.shape
    blk = SB_BLOCK
    grid = (b, SB_WIDTH // LANES, s // blk)
    grid_spec = pltpu.PrefetchScalarGridSpec(
        num_scalar_prefetch=1,
        grid=grid,
        in_specs=[
            pl.BlockSpec((1, blk, LANES), lambda bi, hp, i, *_: (bi, i, hp)),
            pl.BlockSpec((1, s, LANES), lambda bi, hp, i, *_: (bi, 0, hp)),
            pl.BlockSpec((1, s, LANES), lambda bi, hp, i, *_: (bi, 0, hp)),
            pl.BlockSpec((blk, blk), lambda bi, hp, i, *_: (0, 0)),
            pl.BlockSpec((1, 1, LANES), lambda bi, hp, i, *_: (hp, 0, 0)),
        ],
        out_specs=pl.BlockSpec((1, blk, LANES), lambda bi, hp, i, *_: (bi, i, hp)),
        scratch_shapes=[pltpu.VMEM((2 * blk, LANES), F32), pltpu.VMEM((2 * blk, 1), F32)],
    )
    return pl.pallas_call(
        _sb_prompt_kernel,
        grid_spec=grid_spec,
        out_shape=jax.ShapeDtypeStruct((b, s, SB_WIDTH), BF16),
        compiler_params=_cparams(("arbitrary", "arbitrary", "arbitrary")),
        name="sb_prompt",
    )(bias, q, k, v, u, gain)


def _sb_decode_kernel(pt_ref, q_ref, bias_ref, u_ref, g_ref, *refs):
    npg = DEC_PAGES_PER_STEP
    k_refs = refs[:npg]
    v_refs = refs[npg:2 * npg]
    o_ref = refs[2 * npg]
    acc_ref, carry_ref = refs[2 * npg + 1:]
    j = pl.program_id(1)
    rows = 2 * SUBLANES
    page = k_refs[0].shape[3]

    @pl.when(j == 0)
    def _():
        acc_ref[...] = jnp.zeros_like(acc_ref)
        carry_ref[...] = jnp.zeros_like(carry_ref)

    q = jnp.broadcast_to(q_ref[0].astype(F32), (rows, SB_WIDTH))
    rowq = lax.broadcasted_iota(jnp.int32, (rows, SB_WIDTH), 0)
    laneq = lax.broadcasted_iota(jnp.int32, (rows, SB_WIDTH), 1)
    own = (laneq // SB_HEAD_DIM) == rowq
    qbd = jnp.where(own, q, 0.0).astype(BF16)
    zs = []
    for g in range(npg):
        kt = k_refs[g][0].reshape(SB_WIDTH, page).astype(BF16)
        zs.append(jnp.dot(qbd, kt, preferred_element_type=F32) + bias_ref[...])
    z = jnp.concatenate(zs, axis=0)
    cs = jnp.dot(_softplus(z).astype(BF16), u_ref[...], preferred_element_type=F32)
    carry = carry_ref[...]
    acc = acc_ref[...]
    for g in range(npg):
        sl = slice(g * rows, (g + 1) * rows)
        a = _exp((z[sl] - cs[sl]) - carry)
        vt = v_refs[g][0].reshape(SB_WIDTH, page).astype(BF16)
        acc = acc + lax.dot_general(a.astype(BF16), vt, (((1,), (1,)), ((), ())),
                                    preferred_element_type=F32)
        carry = carry + cs[sl, 0:1]
    acc_ref[...] = acc
    carry_ref[...] = carry

    @pl.when(j == pl.num_programs(1) - 1)
    def _():
        om = jnp.where(own, acc, 0.0)
        ms = jnp.sum(om * om, axis=1, keepdims=True) * (1.0 / SB_HEAD_DIM)
        on = om * lax.rsqrt(ms + NORM_EPS)
        o_ref[0] = (jnp.sum(on, axis=0, keepdims=True) * g_ref[...]).astype(BF16)


def _sb_decode(page_table, q, bias_col, u, gain_row, cache_kt, cache_vt):
    bs, n_pages = page_table.shape
    page = cache_kt.shape[3]
    npg = DEC_PAGES_PER_STEP
    assert n_pages % npg == 0
    const = lambda shape: pl.BlockSpec(shape, lambda b, j, pt: (0,) * len(shape))

    def page_spec(g):
        return pl.BlockSpec((1, SB_HEADS, SB_HEAD_DIM, page),
                            lambda b, j, pt: (pt[b, n_pages - 1 - (j * npg + g)], 0, 0, 0))

    grid_spec = pltpu.PrefetchScalarGridSpec(
        num_scalar_prefetch=1,
        grid=(bs, n_pages // npg),
        in_specs=[
            pl.BlockSpec((1, 1, SB_WIDTH), lambda b, j, pt: (b, 0, 0)),
            const((2 * SUBLANES, 1)), const((page, page)), const((1, SB_WIDTH)),
        ] + [page_spec(g) for g in range(npg)] + [page_spec(g) for g in range(npg)],
        out_specs=pl.BlockSpec((1, 1, SB_WIDTH), lambda b, j, pt: (b, 0, 0)),
        scratch_shapes=[pltpu.VMEM((2 * SUBLANES, SB_WIDTH), F32), pltpu.VMEM((2 * SUBLANES, 1), F32)],
    )
    return pl.pallas_call(
        _sb_decode_kernel,
        grid_spec=grid_spec,
        out_shape=jax.ShapeDtypeStruct((bs, 1, SB_WIDTH), BF16),
        compiler_params=_cparams(("arbitrary", "arbitrary")),
        name="sb_decode",
    )(page_table, q, bias_col, u, gain_row, *([cache_kt] * npg), *([cache_vt] * npg))


def _conv_silu(xa, cw, cb, length):
    base = SUBLANES - (CONV_WIDTH - 1)
    out = xa[base:base + length] * cw[0:1] + cb
    for w in range(1, CONV_WIDTH):
        out = out + xa[base + w:base + w + length] * cw[w:w + 1]
    return out * jax.nn.sigmoid(out)


def _mlstm_kernel(qk_ref, v_ref, so_ref, g_ref, gt_ref, cw_ref, cb_ref, og_ref, triu_ref, tril_ref,
                  mix_ref, c_out_ref, n_out_ref, m_out_ref, c_ref, n_ref, m_ref):
    length = ML_CHUNK
    s = qk_ref.shape[1]
    n_chunks = s // length
    c_ref[...] = jnp.zeros_like(c_ref)
    n_ref[...] = jnp.zeros_like(n_ref)
    m_ref[...] = jnp.zeros_like(m_ref)
    rowi = lax.broadcasted_iota(jnp.int32, (length, length), 0)
    coli = lax.broadcasted_iota(jnp.int32, (length, length), 1)
    causal = coli <= rowi
    cw = cw_ref[...]
    cb = cb_ref[...]

    def chunk(c, carry):
        c0 = pl.multiple_of(c * length, length)
        hstart = pl.multiple_of(jnp.maximum(c0 - SUBLANES, 0), SUBLANES)
        hist = qk_ref[0, pl.ds(hstart, SUBLANES), :]
        hist = jnp.where(c > 0, hist, jnp.zeros_like(hist))
        xa = jnp.concatenate([hist, qk_ref[0, pl.ds(c0, length), :]], axis=0)
        act = _conv_silu(xa, cw, cb, length)
        gch = g_ref[0, pl.ds(c0, length), :]
        gtch = gt_ref[c]
        bcum_rows = jnp.dot(gtch, triu_ref[...], precision=HIGHEST, preferred_element_type=F32)
        bcum_cols = jnp.dot(tril_ref[...], gch, precision=HIGHEST, preferred_element_type=F32)
        vch = v_ref[0, pl.ds(c0, length), :]
        soch = so_ref[0, pl.ds(c0, length), :]
        outs = []
        for h in range(ML_HEADS):
            hs = slice(h * ML_HEAD_DIM, (h + 1) * ML_HEAD_DIM)
            qh = act[:, hs]
            kh = act[:, ML_WIDTH + h * ML_HEAD_DIM:ML_WIDTH + (h + 1) * ML_HEAD_DIM] * (ML_HEAD_DIM ** -0.5)
            vh = vch[:, hs]
            u_row = gtch[h:h + 1, :] - bcum_rows[ML_HEADS + h:ML_HEADS + h + 1, :]
            bcum_col = bcum_cols[:, ML_HEADS + h:ML_HEADS + h + 1]
            u_col = gch[:, h:h + 1] - bcum_col
            m_prev = m_ref[h]
            cm = jnp.max(jnp.where(causal, u_row, -jnp.inf), axis=1, keepdims=True)
            mm = jnp.maximum(m_prev, cm)
            d = jnp.where(causal, jnp.exp(u_row - mm), 0.0)
            inter = jnp.exp(m_prev - mm)
            m_t = bcum_col + mm
            qb = qh.astype(BF16)
            kb = kh.astype(BF16)
            w = lax.dot_general(qb, kb, (((1,), (1,)), ((), ())), preferred_element_type=F32) * d
            c_prev = c_ref[h]
            n_prev = n_ref[h]
            num = (jnp.dot(w.astype(BF16), vh, preferred_element_type=F32)
                   + inter * jnp.dot(qb, c_prev.astype(BF16), preferred_element_type=F32))
            den = (jnp.sum(w, axis=1, keepdims=True)
                   + inter * jnp.sum(qh * n_prev, axis=1, keepdims=True))
            hval = num / (jnp.maximum(jnp.abs(den), jnp.exp(-m_t)) + ML_EPS)
            m_end = mm[length - 1:length, :]
            decay = jnp.exp(u_col - m_end)
            scale = inter[length - 1:length, :]
            dk = decay * kh
            c_ref[h] = scale * c_prev + jnp.dot(dk.T.astype(BF16), vh, preferred_element_type=F32)
            n_ref[h] = scale * n_prev + jnp.sum(dk, axis=0, keepdims=True)
            m_ref[h] = m_t[length - 1:length, :]
            hms = jnp.mean(hval * hval, axis=1, keepdims=True)
            hn = (hval * lax.rsqrt(hms + NORM_EPS)) * og_ref[:, hs]
            outs.append(soch[:, hs] * hn)
        mix_ref[0, pl.ds(c0, length), :] = jnp.concatenate(outs, axis=-1).astype(BF16)
        return carry

    lax.fori_loop(0, n_chunks, chunk, 0)
    c_out_ref[0] = c_ref[...]
    n_out_ref[0] = n_ref[...]
    m_out_ref[0] = m_ref[...]


def _mlstm_prompt(qk, v, so, gates, gates_t, conv_w, conv_b, out_g, triu, tril):
    b, s, _ = qk.shape
    full = lambda shape: pl.BlockSpec(shape, lambda i: (0,) * len(shape))
    seq = lambda width: pl.BlockSpec((1, s, width), lambda i: (i, 0, 0))
    out_shape = (
        jax.ShapeDtypeStruct((b, s, ML_WIDTH), BF16),
        jax.ShapeDtypeStruct((b, ML_HEADS, ML_HEAD_DIM, ML_HEAD_DIM), F32),
        jax.ShapeDtypeStruct((b, ML_HEADS, 1, ML_HEAD_DIM), F32),
        jax.ShapeDtypeStruct((b, ML_HEADS, 1, 1), F32),
    )
    out_specs = (
        seq(ML_WIDTH),
        pl.BlockSpec((1, ML_HEADS, ML_HEAD_DIM, ML_HEAD_DIM), lambda i: (i, 0, 0, 0)),
        pl.BlockSpec((1, ML_HEADS, 1, ML_HEAD_DIM), lambda i: (i, 0, 0, 0)),
        pl.BlockSpec((1, ML_HEADS, 1, 1), lambda i: (i, 0, 0, 0)),
    )
    return pl.pallas_call(
        _mlstm_kernel,
        grid=(b,),
        in_specs=[seq(2 * ML_WIDTH), seq(ML_WIDTH), seq(ML_WIDTH), seq(LANES),
                  pl.BlockSpec((s // ML_CHUNK, 2 * GATE_LANES, ML_CHUNK), lambda i: (i, 0, 0)),
                  full((CONV_WIDTH, 2 * ML_WIDTH)), full((1, 2 * ML_WIDTH)), full((1, ML_WIDTH)),
                  full((ML_CHUNK, ML_CHUNK)), full((ML_CHUNK, ML_CHUNK))],
        out_specs=out_specs,
        out_shape=out_shape,
        scratch_shapes=[pltpu.VMEM((ML_HEADS, ML_HEAD_DIM, ML_HEAD_DIM), F32),
                        pltpu.VMEM((ML_HEADS, 1, ML_HEAD_DIM), F32),
                        pltpu.VMEM((ML_HEADS, 1, 1), F32)],
        compiler_params=_cparams(("arbitrary",)),
        name="mlstm_prompt",
    )(qk, v, so, gates, gates_t, conv_w, conv_b, out_g, triu, tril)


def _mlstm_step_kernel(qk_ref, cs_ref, v_ref, so_ref, g_ref, c_ref, n_ref, m_ref,
                       cw_ref, cb_ref, og_ref, eye_ref,
                       mix_ref, c_out_ref, n_out_ref, m_out_ref, conv_out_ref):
    new = qk_ref[0]
    state = cs_ref[0]
    cw = cw_ref[...]
    qc = state[0:1] * cw[0:1] + cb_ref[...]
    for w in range(1, CONV_WIDTH - 1):
        qc = qc + state[w:w + 1] * cw[w:w + 1]
    qc = qc + new * cw[CONV_WIDTH - 1:CONV_WIDTH]
    act = qc * jax.nn.sigmoid(qc)
    conv_out_ref[0] = jnp.concatenate([state[1:], new], axis=0)
    gates = g_ref[0]
    eye = eye_ref[...]
    outs = []
    for h in range(ML_HEADS):
        hs = slice(h * ML_HEAD_DIM, (h + 1) * ML_HEAD_DIM)
        q_row = act[:, hs]
        k_row = act[:, ML_WIDTH + h * ML_HEAD_DIM:ML_WIDTH + (h + 1) * ML_HEAD_DIM] * (ML_HEAD_DIM ** -0.5)
        v_row = v_ref[0][:, hs].astype(F32)
        q_col = jnp.sum(eye * q_row, axis=1, keepdims=True)
        k_col = jnp.sum(eye * k_row, axis=1, keepdims=True)
        ig = gates[:, h:h + 1]
        lf = gates[:, ML_HEADS + h:ML_HEADS + h + 1]
        m_prev = m_ref[0, h]
        c_prev = c_ref[0, h]
        n_prev = n_ref[0, h]
        m_inter = lf + m_prev
        m_t = jnp.maximum(m_inter, ig)
        d = jnp.exp(ig - m_t)
        inter = jnp.exp(m_inter - m_t)
        w = jnp.sum(q_row * k_row, axis=1, keepdims=True) * d
        num = w * v_row + inter * jnp.sum(q_col * c_prev, axis=0, keepdims=True)
        den = w + inter * jnp.sum(q_row * n_prev, axis=1, keepdims=True)
        hval = num / (jnp.maximum(jnp.abs(den), jnp.exp(-m_t)) + ML_EPS)
        c_out_ref[0, h] = inter * c_prev + d * (k_col * v_row)
        n_out_ref[0, h] = inter * n_prev + d * k_row
        m_out_ref[0, h] = m_t
        hms = jnp.mean(hval * hval, axis=1, keepdims=True)
        hn = (hval * lax.rsqrt(hms + NORM_EPS)) * og_ref[:, hs]
        outs.append(so_ref[0][:, hs] * hn)
    mix_ref[0] = jnp.concatenate(outs, axis=-1).astype(BF16)


def _mlstm_sample(qk, conv_state, v, so, gates, c, n, m, conv_w, conv_b, out_g, eye):
    bs = qk.shape[0]
    full = lambda shape: pl.BlockSpec(shape, lambda i: (0,) * len(shape))
    row3 = lambda r, width: pl.BlockSpec((1, r, width), lambda i: (i, 0, 0))
    st4 = lambda a, b2: pl.BlockSpec((1, ML_HEADS, a, b2), lambda i: (i, 0, 0, 0))
    out_shape = (
        jax.ShapeDtypeStruct((bs, 1, ML_WIDTH), BF16),
        jax.ShapeDtypeStruct((bs, ML_HEADS, ML_HEAD_DIM, ML_HEAD_DIM), F32),
        jax.ShapeDtypeStruct((bs, ML_HEADS, 1, ML_HEAD_DIM), F32),
        jax.ShapeDtypeStruct((bs, ML_HEADS, 1, 1), F32),
        jax.ShapeDtypeStruct((bs, CONV_WIDTH - 1, 2 * ML_WIDTH), F32),
    )
    return pl.pallas_call(
        _mlstm_step_kernel,
        grid=(bs,),
        in_specs=[row3(1, 2 * ML_WIDTH), row3(CONV_WIDTH - 1, 2 * ML_WIDTH), row3(1, ML_WIDTH),
                  row3(1, ML_WIDTH), row3(1, LANES),
                  st4(ML_HEAD_DIM, ML_HEAD_DIM), st4(1, ML_HEAD_DIM), st4(1, 1),
                  full((CONV_WIDTH, 2 * ML_WIDTH)), full((1, 2 * ML_WIDTH)), full((1, ML_WIDTH)),
                  full((ML_HEAD_DIM, ML_HEAD_DIM))],
        out_specs=(row3(1, ML_WIDTH), st4(ML_HEAD_DIM, ML_HEAD_DIM), st4(1, ML_HEAD_DIM), st4(1, 1),
                   row3(CONV_WIDTH - 1, 2 * ML_WIDTH)),
        out_shape=out_shape,
        compiler_params=_cparams(("arbitrary",)),
        name="mlstm_sample",
    )(qk, conv_state, v, so, gates, c, n, m, conv_w, conv_b, out_g, eye)


def _mix_router_body(x, sb, ml, wo_ref, gf_ref, wrh_ref, wrl_ref, br_ref, tri_ref, cnt_ref,
                     h_ref, xn_ref, idx_ref, gate_ref, rank_ref):
    rows = x.shape[0]
    h = (x + jnp.dot(sb, wo_ref[:SB_WIDTH, :], preferred_element_type=F32)
         + jnp.dot(ml, wo_ref[SB_WIDTH:, :], preferred_element_type=F32))
    h_ref[0:rows, :] = h
    ms = jnp.mean(h * h, axis=-1, keepdims=True)
    xn = (h * lax.rsqrt(ms + NORM_EPS)) * gf_ref[...]
    xh, xl = _split_bf16(xn)
    bits = pltpu.bitcast(xh.astype(F32), jnp.uint32)
    half = bits.shape[1] // 2
    xn_ref[0:rows, :] = (bits[:, :half] >> 16) | (bits[:, half:] & jnp.uint32(0xFFFF0000))
    logits = (jnp.dot(xh, wrh_ref[...], preferred_element_type=F32)
              + jnp.dot(xl, wrh_ref[...], preferred_element_type=F32)
              + jnp.dot(xh, wrl_ref[...], preferred_element_type=F32)) + br_ref[...]
    lane = lax.broadcasted_iota(jnp.int32, logits.shape, 1).astype(F32)
    neg = -jnp.inf
    logits = jnp.where(lane < N_EXPERTS, logits, neg)
    tops, idxs = [], []
    onehot = jnp.zeros(logits.shape, F32)
    for _ in range(TOP_K):
        mx = jnp.max(logits, axis=1, keepdims=True)
        ix = jnp.min(jnp.where(logits == mx, lane, float(LANES)), axis=1, keepdims=True)
        hit = lane == ix
        onehot = jnp.where(hit, 1.0, onehot)
        logits = jnp.where(hit, neg, logits)
        tops.append(mx)
        idxs.append(ix)
    es = [jnp.exp(t - tops[0]) for t in tops]
    denom = es[0] + es[1] + es[2] + es[3]
    cum = (jnp.dot(tri_ref[0:rows, 0:rows], onehot.astype(BF16), preferred_element_type=F32)
           + cnt_ref[...])
    cnt_ref[...] = cnt_ref[...] + jnp.sum(onehot, axis=0, keepdims=True)
    idx_out = jnp.zeros(logits.shape, F32)
    gate_out = jnp.zeros(logits.shape, F32)
    rank_out = jnp.zeros(logits.shape, F32)
    for k in range(TOP_K):
        rk = jnp.sum(jnp.where(lane == idxs[k], cum, 0.0), axis=1, keepdims=True)
        idx_out = jnp.where(lane == k, idxs[k], idx_out)
        gate_out = jnp.where(lane == k, es[k] / denom, gate_out)
        rank_out = jnp.where(lane == k, rk, rank_out)
    idx_ref[0:rows, :] = idx_out[:, :TOP_K].astype(jnp.int32)
    gate_ref[0:rows, :] = gate_out[:, :TOP_K]
    rank_ref[0:rows, :] = rank_out[:, :TOP_K].astype(jnp.int32)


def _mix_router_kernel(xp_ref, sbp_ref, mlp_ref, xs_ref, sbs_ref, mls_ref,
                       wo_ref, gf_ref, wrh_ref, wrl_ref, br_ref, tri_ref,
                       h_ref, xn_ref, idx_ref, gate_ref, rank_ref, cnt_out_ref, cnt_ref):
    i = pl.program_id(0)
    last = pl.num_programs(0) - 1

    @pl.when(i == 0)
    def _():
        cnt_ref[...] = jnp.zeros_like(cnt_ref)

    common = (wo_ref, gf_ref, wrh_ref, wrl_ref, br_ref, tri_ref, cnt_ref,
              h_ref, xn_ref, idx_ref, gate_ref, rank_ref)

    @pl.when(i < last)
    def _():
        _mix_router_body(xp_ref[...], sbp_ref[...], mlp_ref[...], *common)

    @pl.when(i == last)
    def _():
        _mix_router_body(xs_ref[...], sbs_ref[...], mls_ref[...], *common)
        cnt_out_ref[...] = cnt_ref[...]


def _mix_router(xp, sbp, mlp, xs, sbs, mls, w_out, g_ffn, wr_hi, wr_lo, b_router, tri):
    tp, d = xp.shape
    ts = xs.shape[0]
    rows = ROUTER_ROWS
    assert tp % rows == 0 and ts <= rows
    nblk = tp // rows
    t_all = tp + ts
    full = lambda shape: pl.BlockSpec(shape, lambda i: (0,) * len(shape))
    pblk = lambda width: pl.BlockSpec((rows, width), lambda i: (jnp.minimum(i, nblk - 1), 0))
    oblk = lambda width: pl.BlockSpec((rows, width), lambda i: (i, 0))
    out_shape = (
        jax.ShapeDtypeStruct((t_all, d), F32),
        jax.ShapeDtypeStruct((t_all, d // 2), jnp.uint32),
        jax.ShapeDtypeStruct((t_all, TOP_K), jnp.int32),
        jax.ShapeDtypeStruct((t_all, TOP_K), F32),
        jax.ShapeDtypeStruct((t_all, TOP_K), jnp.int32),
        jax.ShapeDtypeStruct((1, LANES), F32),
    )
    return pl.pallas_call(
        _mix_router_kernel,
        grid=(nblk + 1,),
        in_specs=[pblk(d), pblk(SB_WIDTH), pblk(ML_WIDTH),
                  full((ts, d)), full((ts, SB_WIDTH)), full((ts, ML_WIDTH)),
                  full(w_out.shape), full((1, d)), full((d, LANES)), full((d, LANES)),
                  full((1, LANES)), full((rows, rows))],
        out_specs=(oblk(d), oblk(d // 2), oblk(TOP_K), oblk(TOP_K), oblk(TOP_K), full((1, LANES))),
        out_shape=out_shape,
        scratch_shapes=[pltpu.VMEM((1, LANES), F32)],
        compiler_params=_cparams(("arbitrary",)),
        name="mix_router",
    )(xp, sbp, mlp, xs, sbs, mls, w_out, g_ffn, wr_hi, wr_lo, b_router, tri)


def _for_tokens(n_tok, issue):
    def body(t, c):
        issue(t)
        return c

    lax.fori_loop(0, n_tok, body, 0, unroll=ROW_DMA_UNROLL)


def _block_cases(i, n_tokens, rows, run):
    n_full, rem = divmod(n_tokens, rows)
    if n_full:
        pl.when(i < n_full)(lambda: run(rows))
    if rem:
        pl.when(i == n_full)(lambda: run(rem))


def _dispatch_kernel(dest_ref, xn_ref, buf_in_ref, buf_ref, sem, *, n_tokens):
    del buf_in_ref

    def run(n_tok):
        def issue(t):
            src = xn_ref.at[pl.ds(t, 1)]
            for k in range(TOP_K):
                pltpu.make_async_copy(src, buf_ref.at[pl.ds(dest_ref[t * TOP_K + k], 1)], sem).start()

        _for_tokens(n_tok, issue)
        for _ in range(TOP_K):
            pltpu.make_async_copy(xn_ref.at[pl.ds(0, n_tok)], buf_ref.at[pl.ds(0, n_tok)], sem).wait()

    _block_cases(pl.program_id(0), n_tokens, xn_ref.shape[0], run)


def _dispatch(dest_flat, xn, buf):
    t, d = xn.shape
    rows = DISPATCH_ROWS
    return pl.pallas_call(
        functools.partial(_dispatch_kernel, n_tokens=t),
        grid=(pl.cdiv(t, rows),),
        in_specs=[pl.BlockSpec((rows * TOP_K,), lambda i: (i,), memory_space=pltpu.SMEM),
                  pl.BlockSpec((rows, d), lambda i: (i, 0)),
                  pl.BlockSpec(memory_space=pl.ANY)],
        out_specs=pl.BlockSpec(memory_space=pl.ANY),
        out_shape=jax.ShapeDtypeStruct(buf.shape, buf.dtype),
        scratch_shapes=[pltpu.SemaphoreType.DMA(())],
        input_output_aliases={2: 0},
        compiler_params=_cparams(("arbitrary",)),
        name="dispatch",
    )(dest_flat, xn, buf)


def _experts_kernel(blk_ref, exp_ref, used_ref, x_ref, wu_ref, bu_ref, wd_ref, bd_ref, y_ref):
    i = pl.program_id(0)

    @pl.when(i < used_ref[0])
    def _():
        de = wd_ref.shape[1]
        words = x_ref[...]
        half = words.shape[1]
        x_lo = pltpu.bitcast(words << 16, F32).astype(BF16)
        x_hi = pltpu.bitcast(words & jnp.uint32(0xFFFF0000), F32).astype(BF16)
        hmid = (jnp.dot(x_lo, wu_ref[0, :half, :], preferred_element_type=F32)
                + jnp.dot(x_hi, wu_ref[0, half:, :], preferred_element_type=F32)) + bu_ref[0]
        g = jnp.minimum(hmid[:, :de], SWIGLU_LIMIT)
        lin = jnp.clip(hmid[:, de:], -SWIGLU_LIMIT, SWIGLU_LIMIT)
        act = (lin + 1.0) * g * jax.nn.sigmoid(SWIGLU_ALPHA * g)
        y_ref[...] = jnp.dot(act.astype(BF16), wd_ref[0], preferred_element_type=F32) + bd_ref[0]


def _experts(blk_row, blk_exp, n_used, xbuf, w_up, b_up, w_down, b_down):
    cap = xbuf.shape[0]
    d = w_up.shape[1]
    rows = EXPERT_ROWS
    n_blocks = blk_row.shape[0]
    de2 = w_up.shape[2]
    grid_spec = pltpu.PrefetchScalarGridSpec(
        num_scalar_prefetch=3,
        grid=(n_blocks,),
        in_specs=[
            pl.BlockSpec((rows, d // 2), lambda i, br, be, nu: (br[i], 0)),
            pl.BlockSpec((1, d, de2), lambda i, br, be, nu: (be[i], 0, 0)),
            pl.BlockSpec((1, 1, de2), lambda i, br, be, nu: (be[i], 0, 0)),
            pl.BlockSpec((1, de2 // 2, d), lambda i, br, be, nu: (be[i], 0, 0)),
            pl.BlockSpec((1, 1, d), lambda i, br, be, nu: (be[i], 0, 0)),
        ],
        out_specs=pl.BlockSpec((rows, d), lambda i, br, be, nu: (br[i], 0)),
    )
    return pl.pallas_call(
        _experts_kernel,
        grid_spec=grid_spec,
        out_shape=jax.ShapeDtypeStruct((cap, d), F32),
        compiler_params=_cparams(("arbitrary",)),
        name="experts",
    )(blk_row, blk_exp, n_used, xbuf, w_up, b_up, w_down, b_down)


def _combine_kernel(dest_ref, dest_next_ref, h_ref, gate_ref, y_ref, op_ref, os_ref, rows_ref, sem,
                    *, n_prompt, n_tokens):
    i = pl.program_id(0)
    rows = h_ref.shape[0]
    slot = i & 1

    def fetch(dref, s):
        def run(n_tok):
            def issue(t):
                for k in range(TOP_K):
                    pltpu.make_async_copy(y_ref.at[pl.ds(dref[t * TOP_K + k], 1)],
                                          rows_ref.at[s, k, pl.ds(t, 1)], sem.at[s]).start()

            _for_tokens(n_tok, issue)
        return run

    def drain(n_tok):
        for k in range(TOP_K):
            pltpu.make_async_copy(y_ref.at[pl.ds(0, n_tok)], rows_ref.at[slot, k, pl.ds(0, n_tok)],
                                  sem.at[slot]).wait()

    def combine(n_tok):
        drain(n_tok)
        gates = gate_ref[0:n_tok, :]
        out = h_ref[0:n_tok, :]
        for k in range(TOP_K):
            out = out + gates[:, k:k + 1] * rows_ref[slot, k, 0:n_tok, :]
        return out

    @pl.when(i == 0)
    def _():
        fetch(dest_ref, 0)(rows)

    _block_cases(i + 1, n_tokens, rows, fetch(dest_next_ref, 1 - slot))

    n_pblk = n_prompt // rows

    @pl.when(i < n_pblk)
    def _():
        op_ref[...] = combine(rows)

    @pl.when(i == n_pblk)
    def _():
        os_ref[...] = combine(n_tokens - n_prompt)


def _combine(dest_flat, h, gates, y, n_prompt):
    t, d = h.shape
    rows = COMBINE_ROWS
    n_sample = t - n_prompt
    assert n_prompt % rows == 0 and 0 < n_sample <= rows
    n_steps = n_prompt // rows + 1
    return pl.pallas_call(
        functools.partial(_combine_kernel, n_prompt=n_prompt, n_tokens=t),
        grid=(n_steps,),
        in_specs=[pl.BlockSpec((rows * TOP_K,), lambda i: (i,), memory_space=pltpu.SMEM),
                  pl.BlockSpec((rows * TOP_K,), lambda i: (jnp.minimum(i + 1, n_steps - 1),),
                               memory_space=pltpu.SMEM),
                  pl.BlockSpec((rows, d), lambda i: (i, 0)),
                  pl.BlockSpec((rows, TOP_K), lambda i: (i, 0)),
                  pl.BlockSpec(memory_space=pl.ANY)],
        out_specs=(pl.BlockSpec((rows, d), lambda i: (jnp.minimum(i, n_steps - 2), 0)),
                   pl.BlockSpec((n_sample, d), lambda i: (0, 0))),
        out_shape=(jax.ShapeDtypeStruct((n_prompt, d), F32), jax.ShapeDtypeStruct((n_sample, d), F32)),
        scratch_shapes=[pltpu.VMEM((2, TOP_K, rows, d), F32), pltpu.SemaphoreType.DMA((2,))],
        compiler_params=_cparams(("arbitrary",)),
        name="combine",
    )(dest_flat, dest_flat, h, gates, y)


def _layer(x_prompt, x_sample, cache_k, cache_v, page_table, st_c, st_n, st_m, st_conv,
           norm_mix_g, w_in, b_gates, q_norm_g, k_norm_g, sb_bias, conv_w, conv_b,
           sb_out_g, ml_out_g, w_out, norm_ffn_g, w_router, b_router, w_up, b_up, w_down, b_down):
    bp, sp, d = x_prompt.shape
    bs, ss, _ = x_sample.shape
    assert ss == 1, "decode kernels are written for one new token per sequence"
    tp, ts = bp * sp, bs * ss
    n_main = 3 * SB_WIDTH + 4 * ML_WIDTH

    w_main = w_in[:, :n_main].astype(BF16)
    wg = w_in[:, n_main:]
    w_gate = jnp.pad(wg, ((0, 0), (0, LANES - GATE_LANES))).astype(BF16)
    w_gate_t = jnp.pad(wg.T, ((0, GATE_LANES), (0, 0))).astype(BF16)
    b_gate = jnp.pad(b_gates, (0, LANES - GATE_LANES)).reshape(1, LANES)
    b_gate_t = jnp.pad(b_gates, (0, GATE_LANES)).reshape(2 * GATE_LANES, 1)
    slab = 2 * LANES
    bd = jnp.kron(jnp.eye(slab // SB_HEAD_DIM, dtype=F32),
                  jnp.full((SB_HEAD_DIM, SB_HEAD_DIM), 1.0 / SB_HEAD_DIM, F32)).astype(BF16)
    qk_gain = jnp.concatenate([jnp.tile(q_norm_g, SB_HEADS), jnp.tile(k_norm_g, SB_HEADS)]).reshape(1, -1)
    norm_g = norm_mix_g.reshape(1, d)

    def suffix_ones(n):
        r = jnp.arange(n)
        return (r[:, None] >= r[None, :])

    proj = functools.partial(_proj, norm_g=norm_g, w_main=w_main, w_gate=w_gate, w_gate_t=w_gate_t,
                             bd=bd, qk_gain=qk_gain, b_gate=b_gate, b_gate_t=b_gate_t)
    (q_p, kt_p, kb_p, vt_p, vb_p, mqk_p, mv_p, so_p, gate_p, gatet_p) = proj(
        x_prompt.reshape(tp, d), rows=PROJ_ROWS, seq_len=sp)
    k_p = jnp.transpose(kt_p, (0, 3, 1, 2))
    v_p = jnp.transpose(vt_p, (0, 3, 1, 2))
    (q_s, k_s, _, v_s, _, mqk_s, mv_s, so_s, gate_s, _) = proj(x_sample.reshape(ts, d), rows=ts)

    u_blk = suffix_ones(SB_BLOCK).astype(BF16)
    sb_gain = sb_out_g.reshape(SB_WIDTH // LANES, 1, LANES)
    mix_sb_p = _sb_prompt(sb_bias, q_p.reshape(bp, sp, SB_WIDTH), kb_p.reshape(bp, sp, SB_WIDTH),
                          vb_p.reshape(bp, sp, SB_WIDTH), u_blk, sb_gain)
    tril = suffix_ones(ML_CHUNK).astype(F32)
    triu = tril.T
    ml_gain = ml_out_g.reshape(1, ML_WIDTH)
    conv_b2 = conv_b.reshape(1, 2 * ML_WIDTH)
    gatet_chunks = gatet_p.reshape(2 * GATE_LANES, tp // ML_CHUNK, ML_CHUNK).transpose(1, 0, 2)
    mix_ml_p, c_p, n_p, m_p = _mlstm_prompt(
        mqk_p.reshape(bp, sp, 2 * ML_WIDTH), mv_p.reshape(bp, sp, ML_WIDTH),
        so_p.reshape(bp, sp, ML_WIDTH), gate_p.reshape(bp, sp, LANES), gatet_chunks,
        conv_w, conv_b2, ml_gain, triu, tril)
    conv_p = mqk_p.reshape(bp, sp, 2 * ML_WIDTH)[:, sp - (CONV_WIDTH - 1):]

    page = cache_k.shape[1]
    cache_kt = jnp.transpose(cache_k, (0, 2, 3, 1))
    cache_vt = jnp.transpose(cache_v, (0, 2, 3, 1))
    bias_col = jnp.pad(sb_bias, (0, 2 * SUBLANES - SB_HEADS)).reshape(2 * SUBLANES, 1)
    mix_sb_s = _sb_decode(page_table, q_s.reshape(bs, 1, SB_WIDTH), bias_col,
                          suffix_ones(page).astype(BF16), sb_out_g.reshape(1, SB_WIDTH), cache_kt, cache_vt)
    eye = jnp.eye(ML_HEAD_DIM, dtype=F32)
    mix_ml_s, c_s, n_s, m_s, conv_s = _mlstm_sample(
        mqk_s.reshape(bs, 1, 2 * ML_WIDTH), st_conv, mv_s.reshape(bs, 1, ML_WIDTH),
        so_s.reshape(bs, 1, ML_WIDTH), gate_s.reshape(bs, 1, LANES),
        st_c, st_n.reshape(bs, ML_HEADS, 1, ML_HEAD_DIM), st_m.reshape(bs, ML_HEADS, 1, 1),
        conv_w, conv_b2, ml_gain, eye)

    wr = jnp.pad(w_router, ((0, 0), (0, LANES - N_EXPERTS)))
    wr_hi = wr.astype(BF16)
    wr_lo = (wr - wr_hi.astype(F32)).astype(BF16)
    br = jnp.pad(b_router, (0, LANES - N_EXPERTS)).reshape(1, LANES)
    r = jnp.arange(ROUTER_ROWS)
    tri_strict = (r[None, :] < r[:, None]).astype(BF16)
    h_all, xn_all, idx, gates, rank, counts = _mix_router(
        x_prompt.reshape(tp, d), mix_sb_p.reshape(tp, SB_WIDTH), mix_ml_p.reshape(tp, ML_WIDTH),
        x_sample.reshape(ts, d), mix_sb_s.reshape(ts, SB_WIDTH), mix_ml_s.reshape(ts, ML_WIDTH),
        w_out.astype(BF16), norm_ffn_g.reshape(1, d), wr_hi, wr_lo, br, tri_strict)

    t_all = tp + ts
    rows = EXPERT_ROWS
    cnt = counts[0, :N_EXPERTS].astype(jnp.int32)
    nblk_e = (cnt + rows - 1) // rows
    blk_end = jnp.cumsum(nblk_e)
    blk_start = blk_end - nblk_e
    n_blocks = -(-t_all * TOP_K // rows) + N_EXPERTS
    dest = (blk_start[idx] * rows + rank).reshape(-1)
    step = max(DISPATCH_ROWS, COMBINE_ROWS) * TOP_K
    dest = jnp.pad(dest, (0, -dest.shape[0] % step))
    bid = jnp.arange(n_blocks, dtype=jnp.int32)
    n_used = blk_end[-1]
    bclamp = jnp.minimum(bid, n_used - 1)
    blk_exp = jnp.minimum(jnp.sum(bclamp[:, None] >= blk_end[None, :], axis=1), N_EXPERTS - 1).astype(jnp.int32)
    cap = n_blocks * rows

    xbuf = _dispatch(dest, xn_all, jnp.zeros((cap, d // 2), jnp.uint32))
    y = _experts(bclamp, blk_exp, n_used.reshape(1), xbuf,
                 w_up.astype(BF16), b_up.reshape(N_EXPERTS, 1, -1),
                 w_down.astype(BF16), b_down.reshape(N_EXPERTS, 1, -1))
    out_p, out_s = _combine(dest, h_all, gates, y, tp)

    y_prompt = out_p.reshape(bp, sp, d)
    y_sample = out_s.reshape(bs, ss, d)
    return (y_prompt, y_sample,
            k_p, v_p,
            k_s.reshape(bs, ss, SB_HEADS, SB_HEAD_DIM), v_s.reshape(bs, ss, SB_HEADS, SB_HEAD_DIM),
            c_p, n_p.reshape(bp, ML_HEADS, ML_HEAD_DIM), m_p.reshape(bp, ML_HEADS), conv_p,
            c_s, n_s.reshape(bs, ML_HEADS, ML_HEAD_DIM), m_s.reshape(bs, ML_HEADS), conv_s)


def kernel(x_prompt, x_sample, cache_sb_k, cache_sb_v, page_table, state_mlstm_C, state_mlstm_n, state_mlstm_m, state_mlstm_conv, norm_mix_g, w_in, b_gates, q_norm_g, k_norm_g, sb_bias, conv_w, conv_b, sb_out_g, ml_out_g, w_out, norm_ffn_g, w_router, b_router, w_up, b_up, w_down, b_down):
    depth = w_in.shape[0]
    assert depth == 1, "single-layer step"
    outs = _layer(x_prompt, x_sample, cache_sb_k[0], cache_sb_v[0], page_table,
                  state_mlstm_C[0], state_mlstm_n[0], state_mlstm_m[0], state_mlstm_conv[0],
                  norm_mix_g[0], w_in[0], b_gates[0], q_norm_g[0], k_norm_g[0], sb_bias[0],
                  conv_w[0], conv_b[0], sb_out_g[0], ml_out_g[0], w_out[0], norm_ffn_g[0],
                  w_router[0], b_router[0], w_up[0], b_up[0], w_down[0], b_down[0])
    return tuple(o[None] if i >= 2 else o for i, o in enumerate(outs))
```

```python
import functools

import jax
import jax.numpy as jnp
from jax import lax
from jax.experimental import pallas as pl
from jax.experimental.pallas import tpu as pltpu

F32 = jnp.float32
BF16 = jnp.bfloat16
HIGHEST = lax.Precision.HIGHEST

SB_HEADS = 8
SB_HEAD_DIM = 64
SB_WIDTH = SB_HEADS * SB_HEAD_DIM
SB_SCALE = SB_HEAD_DIM ** -0.5
ML_HEADS = 4
ML_HEAD_DIM = 128
ML_WIDTH = ML_HEADS * ML_HEAD_DIM
ML_EPS = 1e-6
CONV_WIDTH = 4
N_EXPERTS = 32
TOP_K = 4
SWIGLU_LIMIT = 7.0
SWIGLU_ALPHA = 1.702
NORM_EPS = 1e-6
LOG2E = 1.4426950408889634

LANES = 128
SUBLANES = 8
VMEM_LIMIT_BYTES = 56 * 1024 * 1024

PROJ_ROWS = 256
SB_BLOCK = 256
SB_QUERY_BLOCK = 512
ML_CHUNK = 128
DEC_PAGES_PER_STEP = 8
ROUTER_ROWS = 512
EXPERT_ROWS = 512
DISPATCH_ROWS = 512
COMBINE_ROWS = 256
ROW_DMA_UNROLL = 8
GATE_LANES = 2 * ML_HEADS


def _cparams(sem, vmem=VMEM_LIMIT_BYTES):
    return pltpu.CompilerParams(dimension_semantics=sem, vmem_limit_bytes=vmem)


def _softplus(z):
    return jnp.maximum(z, 0.0) + jnp.log(1.0 + jnp.exp2(jnp.abs(z) * (-LOG2E)))


def _exp(x):
    return jnp.exp2(x * LOG2E)


def _split_bf16(x):
    hi = x.astype(BF16)
    lo = (x - hi.astype(F32)).astype(BF16)
    return hi, lo


def _proj_kernel(x_ref, g_ref, w_ref, wg_ref, wgt_ref, bd_ref, qkg_ref, bg_ref, bgt_ref,
                 q_ref, k_ref, kb_ref, v_ref, vb_ref, mqk_ref, mv_ref, so_ref, gate_ref, gatet_ref):
    x = x_ref[...]
    ms = jnp.mean(x * x, axis=-1, keepdims=True)
    xn = (x * lax.rsqrt(ms + NORM_EPS)) * g_ref[...]
    xb = xn.astype(BF16)

    def proj(lo, hi):
        return jnp.dot(xb, w_ref[:, lo:hi], preferred_element_type=F32)

    o1, o2, o3 = SB_WIDTH, 2 * SB_WIDTH, 3 * SB_WIDTH
    pqk = proj(0, o2)
    sq_hi, sq_lo = _split_bf16(pqk * pqk)
    bd = bd_ref[...]
    slab = bd.shape[0]
    parts = []
    for s in range(o2 // slab):
        sl = slice(s * slab, (s + 1) * slab)
        parts.append(jnp.dot(sq_hi[:, sl], bd, preferred_element_type=F32)
                     + jnp.dot(sq_lo[:, sl], bd, preferred_element_type=F32))
    msq = jnp.concatenate(parts, axis=-1)
    qkn = (pqk * lax.rsqrt(msq + NORM_EPS)) * qkg_ref[...]
    def store_cache_rows(ref, val):
        if len(ref.shape) == 4:
            ref[0] = val.T.reshape(SB_HEADS, SB_HEAD_DIM, val.shape[0])
        else:
            ref[...] = val

    q_ref[...] = (qkn[:, :o1] * SB_SCALE).astype(BF16)
    kn = qkn[:, o1:]
    store_cache_rows(k_ref, kn)
    kb_ref[...] = kn.astype(BF16)
    pv = proj(o2, o3)
    store_cache_rows(v_ref, pv)
    vb_ref[...] = pv.astype(BF16)
    mqk_ref[...] = proj(o3, o3 + 2 * ML_WIDTH)
    mv_ref[...] = proj(o3 + 2 * ML_WIDTH, o3 + 3 * ML_WIDTH).astype(BF16)
    so_ref[...] = jax.nn.sigmoid(proj(o3 + 3 * ML_WIDTH, o3 + 4 * ML_WIDTH))

    def gate_act(g, is_ig):
        lf = jnp.minimum(g, 0.0) - jnp.log(1.0 + jnp.exp(-jnp.abs(g)))
        return jnp.where(is_ig, g, lf)

    g = jnp.dot(xb, wg_ref[...], preferred_element_type=F32) + bg_ref[...]
    lane = lax.broadcasted_iota(jnp.int32, g.shape, 1)
    gate_ref[...] = gate_act(g, lane < ML_HEADS)
    gt = lax.dot_general(wgt_ref[...], xb, (((1,), (1,)), ((), ())),
                         preferred_element_type=F32) + bgt_ref[...]
    row = lax.broadcasted_iota(jnp.int32, gt.shape, 0)
    gatet_ref[...] = gate_act(gt, row < ML_HEADS)


def _proj(x2d, norm_g, w_main, w_gate, w_gate_t, bd, qk_gain, b_gate, b_gate_t, rows, seq_len=None):
    t, d = x2d.shape
    n_main = w_main.shape[1]
    full = lambda shape: pl.BlockSpec(shape, lambda i: (0,) * len(shape))
    rowblk = lambda width: pl.BlockSpec((rows, width), lambda i: (i, 0))
    if seq_len is None:
        kv_shape = jax.ShapeDtypeStruct((t, SB_WIDTH), F32)
        kv_spec = rowblk(SB_WIDTH)
    else:
        assert seq_len % rows == 0 and t % seq_len == 0 and rows % LANES == 0
        per_seq = seq_len // rows
        kv_shape = jax.ShapeDtypeStruct((t // seq_len, SB_HEADS, SB_HEAD_DIM, seq_len), F32)
        kv_spec = pl.BlockSpec((1, SB_HEADS, SB_HEAD_DIM, rows),
                               lambda i: (i // per_seq, 0, 0, i % per_seq))
    out_shape = (
        jax.ShapeDtypeStruct((t, SB_WIDTH), BF16),
        kv_shape,
        jax.ShapeDtypeStruct((t, SB_WIDTH), BF16),
        kv_shape,
        jax.ShapeDtypeStruct((t, SB_WIDTH), BF16),
        jax.ShapeDtypeStruct((t, 2 * ML_WIDTH), F32),
        jax.ShapeDtypeStruct((t, ML_WIDTH), BF16),
        jax.ShapeDtypeStruct((t, ML_WIDTH), F32),
        jax.ShapeDtypeStruct((t, LANES), F32),
        jax.ShapeDtypeStruct((2 * GATE_LANES, t), F32),
    )
    out_specs = (
        rowblk(SB_WIDTH), kv_spec, rowblk(SB_WIDTH), kv_spec, rowblk(SB_WIDTH),
        rowblk(2 * ML_WIDTH), rowblk(ML_WIDTH), rowblk(ML_WIDTH), rowblk(LANES),
        pl.BlockSpec((2 * GATE_LANES, rows), lambda i: (0, i)),
    )
    return pl.pallas_call(
        _proj_kernel,
        grid=(t // rows,),
        in_specs=[rowblk(d), full((1, d)), full((d, n_main)), full((d, LANES)),
                  full((2 * GATE_LANES, d)), full(bd.shape), full((1, 2 * SB_WIDTH)),
                  full((1, LANES)), full((2 * GATE_LANES, 1))],
        out_specs=out_specs,
        out_shape=out_shape,
        compiler_params=_cparams(("arbitrary",)),
        name="proj",
    )(x2d, norm_g, w_main, w_gate, w_gate_t, bd, qk_gain, b_gate, b_gate_t)


def _head_norm_pair(o, gain):
    lane = lax.broadcasted_iota(jnp.int32, o.shape, 1)
    first = lane < SB_HEAD_DIM
    sq = o * o
    ms0 = jnp.sum(jnp.where(first, sq, 0.0), axis=1, keepdims=True)
    ms1 = jnp.sum(jnp.where(first, 0.0, sq), axis=1, keepdims=True)
    ms = jnp.where(first, ms0, ms1) * (1.0 / SB_HEAD_DIM)
    return (o * lax.rsqrt(ms + NORM_EPS)) * gain


def _sb_prompt_kernel(bias_ref, q_ref, k_ref, v_ref, u_ref, g_ref, o_ref, acc_ref, carry_ref):
    hp = pl.program_id(1)
    i = pl.program_id(2)
    qb = q_ref.shape[1]
    kb = u_ref.shape[0]
    per = qb // kb
    q = q_ref[0].astype(F32)
    lane = lax.broadcasted_iota(jnp.int32, q.shape, 1)
    first = lane < SB_HEAD_DIM
    qs = jnp.concatenate([jnp.where(first, q, 0.0), jnp.where(first, 0.0, q)], axis=0).astype(BF16)
    rowc = lax.broadcasted_iota(jnp.int32, (2 * qb, 1), 0)
    bias = jnp.where(rowc < qb, bias_ref[2 * hp], bias_ref[2 * hp + 1])
    u = u_ref[...]
    rowi = lax.broadcasted_iota(jnp.int32, (2 * qb, kb), 0)
    coli = lax.broadcasted_iota(jnp.int32, (2 * qb, kb), 1)
    qrow = jnp.where(rowi >= qb, rowi - qb, rowi)

    acc_ref[...] = jnp.zeros_like(acc_ref)
    carry_ref[...] = jnp.zeros_like(carry_ref)

    def step(j, diag_off):
        start = pl.multiple_of(j * kb, kb)
        kblk = k_ref[0, pl.ds(start, kb), :]
        vblk = v_ref[0, pl.ds(start, kb), :]
        z = lax.dot_general(qs, kblk, (((1,), (1,)), ((), ())), preferred_element_type=F32) + bias
        sp = _softplus(z)
        if diag_off is not None:
            causal = coli + diag_off * kb < qrow
            sp = jnp.where(causal, sp, 0.0)
        cs = jnp.dot(sp.astype(BF16), u, preferred_element_type=F32)
        a = _exp((z - cs) - carry_ref[...])
        if diag_off is not None:
            a = jnp.where(causal, a, 0.0)
        acc_ref[...] += jnp.dot(a.astype(BF16), vblk, preferred_element_type=F32)
        carry_ref[...] += cs[:, 0:1]

    for r in range(per - 1, -1, -1):
        step(i * per + r, r)

    def pair(jj, c):
        step(i * per - 1 - 2 * jj, None)
        step(i * per - 2 - 2 * jj, None)
        return c

    lax.fori_loop(0, i * (per // 2), pair, 0)
    o = jnp.where(first, acc_ref[0:qb, :], acc_ref[qb:2 * qb, :])
    o_ref[0] = _head_norm_pair(o, g_ref[0]).astype(BF16)


def _sb_prompt(bias, q, k, v, u, gain):
    b, s, _ = q.shape
    qb = SB_QUERY_BLOCK
    kb = SB_BLOCK
    assert qb % (2 * kb) == 0 and s % qb == 0
    grid = (b, SB_WIDTH // LANES, s // qb)
    grid_spec = pltpu.PrefetchScalarGridSpec(
        num_scalar_prefetch=1,
        grid=grid,
        in_specs=[
            pl.BlockSpec((1, qb, LANES), lambda bi, hp, i, *_: (bi, i, hp)),
            pl.BlockSpec((1, s, LANES), lambda bi, hp, i, *_: (bi, 0, hp)),
            pl.BlockSpec((1, s, LANES), lambda bi, hp, i, *_: (bi, 0, hp)),
            pl.BlockSpec((kb, kb), lambda bi, hp, i, *_: (0, 0)),
            pl.BlockSpec((1, 1, LANES), lambda bi, hp, i, *_: (hp, 0, 0)),
        ],
        out_specs=pl.BlockSpec((1, qb, LANES), lambda bi, hp, i, *_: (bi, i, hp)),
        scratch_shapes=[pltpu.VMEM((2 * qb, LANES), F32), pltpu.VMEM((2 * qb, 1), F32)],
    )
    return pl.pallas_call(
        _sb_prompt_kernel,
        grid_spec=grid_spec,
        out_shape=jax.ShapeDtypeStruct((b, s, SB_WIDTH), BF16),
        compiler_params=_cparams(("arbitrary", "arbitrary", "arbitrary")),
        name="sb_prompt",
    )(bias, q, k, v, u, gain)


def _sb_decode_kernel(pt_ref, q_ref, bias_ref, u_ref, g_ref, *refs):
    npg = DEC_PAGES_PER_STEP
    k_refs = refs[:npg]
    v_refs = refs[npg:2 * npg]
    o_ref = refs[2 * npg]
    acc_ref, carry_ref = refs[2 * npg + 1:]
    j = pl.program_id(1)
    rows = 2 * SUBLANES
    page = k_refs[0].shape[3]

    @pl.when(j == 0)
    def _():
        acc_ref[...] = jnp.zeros_like(acc_ref)
        carry_ref[...] = jnp.zeros_like(carry_ref)

    q = jnp.broadcast_to(q_ref[0].astype(F32), (rows, SB_WIDTH))
    rowq = lax.broadcasted_iota(jnp.int32, (rows, SB_WIDTH), 0)
    laneq = lax.broadcasted_iota(jnp.int32, (rows, SB_WIDTH), 1)
    own = (laneq // SB_HEAD_DIM) == rowq
    qbd = jnp.where(own, q, 0.0).astype(BF16)
    zs = []
    for g in range(npg):
        kt = k_refs[g][0].reshape(SB_WIDTH, page).astype(BF16)
        zs.append(jnp.dot(qbd, kt, preferred_element_type=F32) + bias_ref[...])
    z = jnp.concatenate(zs, axis=0)
    cs = jnp.dot(_softplus(z).astype(BF16), u_ref[...], preferred_element_type=F32)
    carry = carry_ref[...]
    acc = acc_ref[...]
    for g in range(npg):
        sl = slice(g * rows, (g + 1) * rows)
        a = _exp((z[sl] - cs[sl]) - carry)
        vt = v_refs[g][0].reshape(SB_WIDTH, page).astype(BF16)
        acc = acc + lax.dot_general(a.astype(BF16), vt, (((1,), (1,)), ((), ())),
                                    preferred_element_type=F32)
        carry = carry + cs[sl, 0:1]
    acc_ref[...] = acc
    carry_ref[...] = carry

    @pl.when(j == pl.num_programs(1) - 1)
    def _():
        om = jnp.where(own, acc, 0.0)
        ms = jnp.sum(om * om, axis=1, keepdims=True) * (1.0 / SB_HEAD_DIM)
        on = om * lax.rsqrt(ms + NORM_EPS)
        o_ref[0] = (jnp.sum(on, axis=0, keepdims=True) * g_ref[...]).astype(BF16)


def _sb_decode(page_table, q, bias_col, u, gain_row, cache_kt, cache_vt):
    bs, n_pages = page_table.shape
    page = cache_kt.shape[3]
    npg = DEC_PAGES_PER_STEP
    assert n_pages % npg == 0
    const = lambda shape: pl.BlockSpec(shape, lambda b, j, pt: (0,) * len(shape))

    def page_spec(g):
        return pl.BlockSpec((1, SB_HEADS, SB_HEAD_DIM, page),
                            lambda b, j, pt: (pt[b, n_pages - 1 - (j * npg + g)], 0, 0, 0))

    grid_spec = pltpu.PrefetchScalarGridSpec(
        num_scalar_prefetch=1,
        grid=(bs, n_pages // npg),
        in_specs=[
            pl.BlockSpec((1, 1, SB_WIDTH), lambda b, j, pt: (b, 0, 0)),
            const((2 * SUBLANES, 1)), const((page, page)), const((1, SB_WIDTH)),
        ] + [page_spec(g) for g in range(npg)] + [page_spec(g) for g in range(npg)],
        out_specs=pl.BlockSpec((1, 1, SB_WIDTH), lambda b, j, pt: (b, 0, 0)),
        scratch_shapes=[pltpu.VMEM((2 * SUBLANES, SB_WIDTH), F32), pltpu.VMEM((2 * SUBLANES, 1), F32)],
    )
    return pl.pallas_call(
        _sb_decode_kernel,
        grid_spec=grid_spec,
        out_shape=jax.ShapeDtypeStruct((bs, 1, SB_WIDTH), BF16),
        compiler_params=_cparams(("arbitrary", "arbitrary")),
        name="sb_decode",
    )(page_table, q, bias_col, u, gain_row, *([cache_kt] * npg), *([cache_vt] * npg))


def _conv_silu(xa, cw, cb, length):
    base = SUBLANES - (CONV_WIDTH - 1)
    out = xa[base:base + length] * cw[0:1] + cb
    for w in range(1, CONV_WIDTH):
        out = out + xa[base + w:base + w + length] * cw[w:w + 1]
    return out * jax.nn.sigmoid(out)


def _mlstm_kernel(qk_ref, v_ref, so_ref, g_ref, gt_ref, cw_ref, cb_ref, og_ref, triu_ref, tril_ref,
                  mix_ref, c_out_ref, n_out_ref, m_out_ref, c_ref, n_ref, m_ref):
    length = ML_CHUNK
    s = qk_ref.shape[1]
    n_chunks = s // length
    c_ref[...] = jnp.zeros_like(c_ref)
    n_ref[...] = jnp.zeros_like(n_ref)
    m_ref[...] = jnp.zeros_like(m_ref)
    rowi = lax.broadcasted_iota(jnp.int32, (length, length), 0)
    coli = lax.broadcasted_iota(jnp.int32, (length, length), 1)
    causal = coli <= rowi
    cw = cw_ref[...]
    cb = cb_ref[...]

    def chunk(c, carry):
        c0 = pl.multiple_of(c * length, length)
        hstart = pl.multiple_of(jnp.maximum(c0 - SUBLANES, 0), SUBLANES)
        hist = qk_ref[0, pl.ds(hstart, SUBLANES), :]
        hist = jnp.where(c > 0, hist, jnp.zeros_like(hist))
        xa = jnp.concatenate([hist, qk_ref[0, pl.ds(c0, length), :]], axis=0)
        act = _conv_silu(xa, cw, cb, length)
        gch = g_ref[0, pl.ds(c0, length), :]
        gtch = gt_ref[c]
        bcum_rows = jnp.dot(gtch, triu_ref[...], precision=HIGHEST, preferred_element_type=F32)
        bcum_cols = jnp.dot(tril_ref[...], gch, precision=HIGHEST, preferred_element_type=F32)
        vch = v_ref[0, pl.ds(c0, length), :]
        soch = so_ref[0, pl.ds(c0, length), :]
        outs = []
        for h in range(ML_HEADS):
            hs = slice(h * ML_HEAD_DIM, (h + 1) * ML_HEAD_DIM)
            qh = act[:, hs]
            kh = act[:, ML_WIDTH + h * ML_HEAD_DIM:ML_WIDTH + (h + 1) * ML_HEAD_DIM] * (ML_HEAD_DIM ** -0.5)
            vh = vch[:, hs]
            u_row = gtch[h:h + 1, :] - bcum_rows[ML_HEADS + h:ML_HEADS + h + 1, :]
            bcum_col = bcum_cols[:, ML_HEADS + h:ML_HEADS + h + 1]
            u_col = gch[:, h:h + 1] - bcum_col
            m_prev = m_ref[h]
            cm = jnp.max(jnp.where(causal, u_row, -jnp.inf), axis=1, keepdims=True)
            mm = jnp.maximum(m_prev, cm)
            d = jnp.where(causal, jnp.exp(u_row - mm), 0.0)
            inter = jnp.exp(m_prev - mm)
            m_t = bcum_col + mm
            qb = qh.astype(BF16)
            kb = kh.astype(BF16)
            w = lax.dot_general(qb, kb, (((1,), (1,)), ((), ())), preferred_element_type=F32) * d
            c_prev = c_ref[h]
            n_prev = n_ref[h]
            num = (jnp.dot(w.astype(BF16), vh, preferred_element_type=F32)
                   + inter * jnp.dot(qb, c_prev.astype(BF16), preferred_element_type=F32))
            den = (jnp.sum(w, axis=1, keepdims=True)
                   + inter * jnp.sum(qh * n_prev, axis=1, keepdims=True))
            hval = num / (jnp.maximum(jnp.abs(den), jnp.exp(-m_t)) + ML_EPS)
            m_end = mm[length - 1:length, :]
            decay = jnp.exp(u_col - m_end)
            scale = inter[length - 1:length, :]
            dk = decay * kh
            c_ref[h] = scale * c_prev + jnp.dot(dk.T.astype(BF16), vh, preferred_element_type=F32)
            n_ref[h] = scale * n_prev + jnp.sum(dk, axis=0, keepdims=True)
            m_ref[h] = m_t[length - 1:length, :]
            hms = jnp.mean(hval * hval, axis=1, keepdims=True)
            hn = (hval * lax.rsqrt(hms + NORM_EPS)) * og_ref[:, hs]
            outs.append(soch[:, hs] * hn)
        mix_ref[0, pl.ds(c0, length), :] = jnp.concatenate(outs, axis=-1).astype(BF16)
        return carry

    lax.fori_loop(0, n_chunks, chunk, 0)
    c_out_ref[0] = c_ref[...]
    n_out_ref[0] = n_ref[...]
    m_out_ref[0] = m_ref[...]


def _mlstm_prompt(qk, v, so, gates, gates_t, conv_w, conv_b, out_g, triu, tril):
    b, s, _ = qk.shape
    full = lambda shape: pl.BlockSpec(shape, lambda i: (0,) * len(shape))
    seq = lambda width: pl.BlockSpec((1, s, width), lambda i: (i, 0, 0))
    out_shape = (
        jax.ShapeDtypeStruct((b, s, ML_WIDTH), BF16),
        jax.ShapeDtypeStruct((b, ML_HEADS, ML_HEAD_DIM, ML_HEAD_DIM), F32),
        jax.ShapeDtypeStruct((b, ML_HEADS, 1, ML_HEAD_DIM), F32),
        jax.ShapeDtypeStruct((b, ML_HEADS, 1, 1), F32),
    )
    out_specs = (
        seq(ML_WIDTH),
        pl.BlockSpec((1, ML_HEADS, ML_HEAD_DIM, ML_HEAD_DIM), lambda i: (i, 0, 0, 0)),
        pl.BlockSpec((1, ML_HEADS, 1, ML_HEAD_DIM), lambda i: (i, 0, 0, 0)),
        pl.BlockSpec((1, ML_HEADS, 1, 1), lambda i: (i, 0, 0, 0)),
    )
    return pl.pallas_call(
        _mlstm_kernel,
        grid=(b,),
        in_specs=[seq(2 * ML_WIDTH), seq(ML_WIDTH), seq(ML_WIDTH), seq(LANES),
                  pl.BlockSpec((s // ML_CHUNK, 2 * GATE_LANES, ML_CHUNK), lambda i: (i, 0, 0)),
                  full((CONV_WIDTH, 2 * ML_WIDTH)), full((1, 2 * ML_WIDTH)), full((1, ML_WIDTH)),
                  full((ML_CHUNK, ML_CHUNK)), full((ML_CHUNK, ML_CHUNK))],
        out_specs=out_specs,
        out_shape=out_shape,
        scratch_shapes=[pltpu.VMEM((ML_HEADS, ML_HEAD_DIM, ML_HEAD_DIM), F32),
                        pltpu.VMEM((ML_HEADS, 1, ML_HEAD_DIM), F32),
                        pltpu.VMEM((ML_HEADS, 1, 1), F32)],
        compiler_params=_cparams(("arbitrary",)),
        name="mlstm_prompt",
    )(qk, v, so, gates, gates_t, conv_w, conv_b, out_g, triu, tril)


def _mlstm_step_kernel(qk_ref, cs_ref, v_ref, so_ref, g_ref, c_ref, n_ref, m_ref,
                       cw_ref, cb_ref, og_ref, eye_ref,
                       mix_ref, c_out_ref, n_out_ref, m_out_ref, conv_out_ref):
    new = qk_ref[0]
    state = cs_ref[0]
    cw = cw_ref[...]
    qc = state[0:1] * cw[0:1] + cb_ref[...]
    for w in range(1, CONV_WIDTH - 1):
        qc = qc + state[w:w + 1] * cw[w:w + 1]
    qc = qc + new * cw[CONV_WIDTH - 1:CONV_WIDTH]
    act = qc * jax.nn.sigmoid(qc)
    conv_out_ref[0] = jnp.concatenate([state[1:], new], axis=0)
    gates = g_ref[0]
    eye = eye_ref[...]
    outs = []
    for h in range(ML_HEADS):
        hs = slice(h * ML_HEAD_DIM, (h + 1) * ML_HEAD_DIM)
        q_row = act[:, hs]
        k_row = act[:, ML_WIDTH + h * ML_HEAD_DIM:ML_WIDTH + (h + 1) * ML_HEAD_DIM] * (ML_HEAD_DIM ** -0.5)
        v_row = v_ref[0][:, hs].astype(F32)
        q_col = jnp.sum(eye * q_row, axis=1, keepdims=True)
        k_col = jnp.sum(eye * k_row, axis=1, keepdims=True)
        ig = gates[:, h:h + 1]
        lf = gates[:, ML_HEADS + h:ML_HEADS + h + 1]
        m_prev = m_ref[0, h]
        c_prev = c_ref[0, h]
        n_prev = n_ref[0, h]
        m_inter = lf + m_prev
        m_t = jnp.maximum(m_inter, ig)
        d = jnp.exp(ig - m_t)
        inter = jnp.exp(m_inter - m_t)
        w = jnp.sum(q_row * k_row, axis=1, keepdims=True) * d
        num = w * v_row + inter * jnp.sum(q_col * c_prev, axis=0, keepdims=True)
        den = w + inter * jnp.sum(q_row * n_prev, axis=1, keepdims=True)
        hval = num / (jnp.maximum(jnp.abs(den), jnp.exp(-m_t)) + ML_EPS)
        c_out_ref[0, h] = inter * c_prev + d * (k_col * v_row)
        n_out_ref[0, h] = inter * n_prev + d * k_row
        m_out_ref[0, h] = m_t
        hms = jnp.mean(hval * hval, axis=1, keepdims=True)
        hn = (hval * lax.rsqrt(hms + NORM_EPS)) * og_ref[:, hs]
        outs.append(so_ref[0][:, hs] * hn)
    mix_ref[0] = jnp.concatenate(outs, axis=-1).astype(BF16)


def _mlstm_sample(qk, conv_state, v, so, gates, c, n, m, conv_w, conv_b, out_g, eye):
    bs = qk.shape[0]
    full = lambda shape: pl.BlockSpec(shape, lambda i: (0,) * len(shape))
    row3 = lambda r, width: pl.BlockSpec((1, r, width), lambda i: (i, 0, 0))
    st4 = lambda a, b2: pl.BlockSpec((1, ML_HEADS, a, b2), lambda i: (i, 0, 0, 0))
    out_shape = (
        jax.ShapeDtypeStruct((bs, 1, ML_WIDTH), BF16),
        jax.ShapeDtypeStruct((bs, ML_HEADS, ML_HEAD_DIM, ML_HEAD_DIM), F32),
        jax.ShapeDtypeStruct((bs, ML_HEADS, 1, ML_HEAD_DIM), F32),
        jax.ShapeDtypeStruct((bs, ML_HEADS, 1, 1), F32),
        jax.ShapeDtypeStruct((bs, CONV_WIDTH - 1, 2 * ML_WIDTH), F32),
    )
    return pl.pallas_call(
        _mlstm_step_kernel,
        grid=(bs,),
        in_specs=[row3(1, 2 * ML_WIDTH), row3(CONV_WIDTH - 1, 2 * ML_WIDTH), row3(1, ML_WIDTH),
                  row3(1, ML_WIDTH), row3(1, LANES),
                  st4(ML_HEAD_DIM, ML_HEAD_DIM), st4(1, ML_HEAD_DIM), st4(1, 1),
                  full((CONV_WIDTH, 2 * ML_WIDTH)), full((1, 2 * ML_WIDTH)), full((1, ML_WIDTH)),
                  full((ML_HEAD_DIM, ML_HEAD_DIM))],
        out_specs=(row3(1, ML_WIDTH), st4(ML_HEAD_DIM, ML_HEAD_DIM), st4(1, ML_HEAD_DIM), st4(1, 1),
                   row3(CONV_WIDTH - 1, 2 * ML_WIDTH)),
        out_shape=out_shape,
        compiler_params=_cparams(("arbitrary",)),
        name="mlstm_sample",
    )(qk, conv_state, v, so, gates, c, n, m, conv_w, conv_b, out_g, eye)


def _mix_router_body(x, sb, ml, wo_ref, gf_ref, wrh_ref, wrl_ref, br_ref, tri_ref, cnt_ref,
                     h_ref, xn_ref, idx_ref, gate_ref, rank_ref):
    rows = x.shape[0]
    h = (x + jnp.dot(sb, wo_ref[:SB_WIDTH, :], preferred_element_type=F32)
         + jnp.dot(ml, wo_ref[SB_WIDTH:, :], preferred_element_type=F32))
    h_ref[0:rows, :] = h
    ms = jnp.mean(h * h, axis=-1, keepdims=True)
    xn = (h * lax.rsqrt(ms + NORM_EPS)) * gf_ref[...]
    xh, xl = _split_bf16(xn)
    bits = pltpu.bitcast(xh.astype(F32), jnp.uint32)
    half = bits.shape[1] // 2
    xn_ref[0:rows, :] = (bits[:, :half] >> 16) | (bits[:, half:] & jnp.uint32(0xFFFF0000))
    logits = (jnp.dot(xh, wrh_ref[...], preferred_element_type=F32)
              + jnp.dot(xl, wrh_ref[...], preferred_element_type=F32)
              + jnp.dot(xh, wrl_ref[...], preferred_element_type=F32)) + br_ref[...]
    lane = lax.broadcasted_iota(jnp.int32, logits.shape, 1).astype(F32)
    neg = -jnp.inf
    logits = jnp.where(lane < N_EXPERTS, logits, neg)
    tops, idxs = [], []
    onehot = jnp.zeros(logits.shape, F32)
    for _ in range(TOP_K):
        mx = jnp.max(logits, axis=1, keepdims=True)
        ix = jnp.min(jnp.where(logits == mx, lane, float(LANES)), axis=1, keepdims=True)
        hit = lane == ix
        onehot = jnp.where(hit, 1.0, onehot)
        logits = jnp.where(hit, neg, logits)
        tops.append(mx)
        idxs.append(ix)
    es = [jnp.exp(t - tops[0]) for t in tops]
    denom = es[0] + es[1] + es[2] + es[3]
    cum = (jnp.dot(tri_ref[0:rows, 0:rows], onehot.astype(BF16), preferred_element_type=F32)
           + cnt_ref[...])
    cnt_ref[...] = cnt_ref[...] + jnp.sum(onehot, axis=0, keepdims=True)
    idx_out = jnp.zeros(logits.shape, F32)
    gate_out = jnp.zeros(logits.shape, F32)
    rank_out = jnp.zeros(logits.shape, F32)
    for k in range(TOP_K):
        rk = jnp.sum(jnp.where(lane == idxs[k], cum, 0.0), axis=1, keepdims=True)
        idx_out = jnp.where(lane == k, idxs[k], idx_out)
        gate_out = jnp.where(lane == k, es[k] / denom, gate_out)
        rank_out = jnp.where(lane == k, rk, rank_out)
    idx_ref[0:rows, :] = idx_out[:, :TOP_K].astype(jnp.int32)
    gate_ref[0:rows, :] = gate_out[:, :TOP_K]
    rank_ref[0:rows, :] = rank_out[:, :TOP_K].astype(jnp.int32)


def _mix_router_kernel(xp_ref, sbp_ref, mlp_ref, xs_ref, sbs_ref, mls_ref,
                       wo_ref, gf_ref, wrh_ref, wrl_ref, br_ref, tri_ref,
                       h_ref, xn_ref, idx_ref, gate_ref, rank_ref, cnt_out_ref, cnt_ref):
    i = pl.program_id(0)
    last = pl.num_programs(0) - 1

    @pl.when(i == 0)
    def _():
        cnt_ref[...] = jnp.zeros_like(cnt_ref)

    common = (wo_ref, gf_ref, wrh_ref, wrl_ref, br_ref, tri_ref, cnt_ref,
              h_ref, xn_ref, idx_ref, gate_ref, rank_ref)

    @pl.when(i < last)
    def _():
        _mix_router_body(xp_ref[...], sbp_ref[...], mlp_ref[...], *common)

    @pl.when(i == last)
    def _():
        _mix_router_body(xs_ref[...], sbs_ref[...], mls_ref[...], *common)
        cnt_out_ref[...] = cnt_ref[...]


def _mix_router(xp, sbp, mlp, xs, sbs, mls, w_out, g_ffn, wr_hi, wr_lo, b_router, tri):
    tp, d = xp.shape
    ts = xs.shape[0]
    rows = ROUTER_ROWS
    assert tp % rows == 0 and ts <= rows
    nblk = tp // rows
    t_all = tp + ts
    full = lambda shape: pl.BlockSpec(shape, lambda i: (0,) * len(shape))
    pblk = lambda width: pl.BlockSpec((rows, width), lambda i: (jnp.minimum(i, nblk - 1), 0))
    oblk = lambda width: pl.BlockSpec((rows, width), lambda i: (i, 0))
    out_shape = (
        jax.ShapeDtypeStruct((t_all, d), F32),
        jax.ShapeDtypeStruct((t_all, d // 2), jnp.uint32),
        jax.ShapeDtypeStruct((t_all, TOP_K), jnp.int32),
        jax.ShapeDtypeStruct((t_all, TOP_K), F32),
        jax.ShapeDtypeStruct((t_all, TOP_K), jnp.int32),
        jax.ShapeDtypeStruct((1, LANES), F32),
    )
    return pl.pallas_call(
        _mix_router_kernel,
        grid=(nblk + 1,),
        in_specs=[pblk(d), pblk(SB_WIDTH), pblk(ML_WIDTH),
                  full((ts, d)), full((ts, SB_WIDTH)), full((ts, ML_WIDTH)),
                  full(w_out.shape), full((1, d)), full((d, LANES)), full((d, LANES)),
                  full((1, LANES)), full((rows, rows))],
        out_specs=(oblk(d), oblk(d // 2), oblk(TOP_K), oblk(TOP_K), oblk(TOP_K), full((1, LANES))),
        out_shape=out_shape,
        scratch_shapes=[pltpu.VMEM((1, LANES), F32)],
        compiler_params=_cparams(("arbitrary",)),
        name="mix_router",
    )(xp, sbp, mlp, xs, sbs, mls, w_out, g_ffn, wr_hi, wr_lo, b_router, tri)


def _for_tokens(n_tok, issue):
    def body(t, c):
        issue(t)
        return c

    lax.fori_loop(0, n_tok, body, 0, unroll=ROW_DMA_UNROLL)


def _block_cases(i, n_tokens, rows, run):
    n_full, rem = divmod(n_tokens, rows)
    if n_full:
        pl.when(i < n_full)(lambda: run(rows))
    if rem:
        pl.when(i == n_full)(lambda: run(rem))


def _dispatch_kernel(dest_ref, xn_ref, buf_in_ref, buf_ref, sem, *, n_tokens):
    del buf_in_ref

    def run(n_tok):
        def issue(t):
            src = xn_ref.at[pl.ds(t, 1)]
            for k in range(TOP_K):
                pltpu.make_async_copy(src, buf_ref.at[pl.ds(dest_ref[t * TOP_K + k], 1)], sem).start()

        _for_tokens(n_tok, issue)
        for _ in range(TOP_K):
            pltpu.make_async_copy(xn_ref.at[pl.ds(0, n_tok)], buf_ref.at[pl.ds(0, n_tok)], sem).wait()

    _block_cases(pl.program_id(0), n_tokens, xn_ref.shape[0], run)


def _dispatch(dest_flat, xn, buf):
    t, d = xn.shape
    rows = DISPATCH_ROWS
    return pl.pallas_call(
        functools.partial(_dispatch_kernel, n_tokens=t),
        grid=(pl.cdiv(t, rows),),
        in_specs=[pl.BlockSpec((rows * TOP_K,), lambda i: (i,), memory_space=pltpu.SMEM),
                  pl.BlockSpec((rows, d), lambda i: (i, 0)),
                  pl.BlockSpec(memory_space=pl.ANY)],
        out_specs=pl.BlockSpec(memory_space=pl.ANY),
        out_shape=jax.ShapeDtypeStruct(buf.shape, buf.dtype),
        scratch_shapes=[pltpu.SemaphoreType.DMA(())],
        input_output_aliases={2: 0},
        compiler_params=_cparams(("arbitrary",)),
        name="dispatch",
    )(dest_flat, xn, buf)


def _experts_kernel(blk_ref, exp_ref, used_ref, x_ref, wu_ref, bu_ref, wd_ref, bd_ref, y_ref):
    i = pl.program_id(0)

    @pl.when(i < used_ref[0])
    def _():
        de = wd_ref.shape[1]
        words = x_ref[...]
        half = words.shape[1]
        x_lo = pltpu.bitcast(words << 16, F32).astype(BF16)
        x_hi = pltpu.bitcast(words & jnp.uint32(0xFFFF0000), F32).astype(BF16)
        hmid = (jnp.dot(x_lo, wu_ref[0, :half, :], preferred_element_type=F32)
                + jnp.dot(x_hi, wu_ref[0, half:, :], preferred_element_type=F32)) + bu_ref[0]
        g = jnp.minimum(hmid[:, :de], SWIGLU_LIMIT)
        lin = jnp.clip(hmid[:, de:], -SWIGLU_LIMIT, SWIGLU_LIMIT)
        act = (lin + 1.0) * g * jax.nn.sigmoid(SWIGLU_ALPHA * g)
        y_ref[...] = jnp.dot(act.astype(BF16), wd_ref[0], preferred_element_type=F32) + bd_ref[0]


def _experts(blk_row, blk_exp, n_used, xbuf, w_up, b_up, w_down, b_down):
    cap = xbuf.shape[0]
    d = w_up.shape[1]
    rows = EXPERT_ROWS
    n_blocks = blk_row.shape[0]
    de2 = w_up.shape[2]
    grid_spec = pltpu.PrefetchScalarGridSpec(
        num_scalar_prefetch=3,
        grid=(n_blocks,),
        in_specs=[
            pl.BlockSpec((rows, d // 2), lambda i, br, be, nu: (br[i], 0)),
            pl.BlockSpec((1, d, de2), lambda i, br, be, nu: (be[i], 0, 0)),
            pl.BlockSpec((1, 1, de2), lambda i, br, be, nu: (be[i], 0, 0)),
            pl.BlockSpec((1, de2 // 2, d), lambda i, br, be, nu: (be[i], 0, 0)),
            pl.BlockSpec((1, 1, d), lambda i, br, be, nu: (be[i], 0, 0)),
        ],
        out_specs=pl.BlockSpec((rows, d), lambda i, br, be, nu: (br[i], 0)),
    )
    return pl.pallas_call(
        _experts_kernel,
        grid_spec=grid_spec,
        out_shape=jax.ShapeDtypeStruct((cap, d), F32),
        compiler_params=_cparams(("arbitrary",)),
        name="experts",
    )(blk_row, blk_exp, n_used, xbuf, w_up, b_up, w_down, b_down)


def _combine_kernel(dest_ref, dest_next_ref, h_ref, gate_ref, y_ref, op_ref, os_ref, rows_ref, sem,
                    *, n_prompt, n_tokens):
    i = pl.program_id(0)
    rows = h_ref.shape[0]
    slot = i & 1

    def fetch(dref, s):
        def run(n_tok):
            def issue(t):
                for k in range(TOP_K):
                    pltpu.make_async_copy(y_ref.at[pl.ds(dref[t * TOP_K + k], 1)],
                                          rows_ref.at[s, k, pl.ds(t, 1)], sem.at[s]).start()

            _for_tokens(n_tok, issue)
        return run

    def drain(n_tok):
        for k in range(TOP_K):
            pltpu.make_async_copy(y_ref.at[pl.ds(0, n_tok)], rows_ref.at[slot, k, pl.ds(0, n_tok)],
                                  sem.at[slot]).wait()

    def combine(n_tok):
        drain(n_tok)
        gates = gate_ref[0:n_tok, :]
        out = h_ref[0:n_tok, :]
        for k in range(TOP_K):
            out = out + gates[:, k:k + 1] * rows_ref[slot, k, 0:n_tok, :]
        return out

    @pl.when(i == 0)
    def _():
        fetch(dest_ref, 0)(rows)

    _block_cases(i + 1, n_tokens, rows, fetch(dest_next_ref, 1 - slot))

    n_pblk = n_prompt // rows

    @pl.when(i < n_pblk)
    def _():
        op_ref[...] = combine(rows)

    @pl.when(i == n_pblk)
    def _():
        os_ref[...] = combine(n_tokens - n_prompt)


def _combine(dest_flat, h, gates, y, n_prompt):
    t, d = h.shape
    rows = COMBINE_ROWS
    n_sample = t - n_prompt
    assert n_prompt % rows == 0 and 0 < n_sample <= rows
    n_steps = n_prompt // rows + 1
    return pl.pallas_call(
        functools.partial(_combine_kernel, n_prompt=n_prompt, n_tokens=t),
        grid=(n_steps,),
        in_specs=[pl.BlockSpec((rows * TOP_K,), lambda i: (i,), memory_space=pltpu.SMEM),
                  pl.BlockSpec((rows * TOP_K,), lambda i: (jnp.minimum(i + 1, n_steps - 1),),
                               memory_space=pltpu.SMEM),
                  pl.BlockSpec((rows, d), lambda i: (i, 0)),
                  pl.BlockSpec((rows, TOP_K), lambda i: (i, 0)),
                  pl.BlockSpec(memory_space=pl.ANY)],
        out_specs=(pl.BlockSpec((rows, d), lambda i: (jnp.minimum(i, n_steps - 2), 0)),
                   pl.BlockSpec((n_sample, d), lambda i: (0, 0))),
        out_shape=(jax.ShapeDtypeStruct((n_prompt, d), F32), jax.ShapeDtypeStruct((n_sample, d), F32)),
        scratch_shapes=[pltpu.VMEM((2, TOP_K, rows, d), F32), pltpu.SemaphoreType.DMA((2,))],
        compiler_params=_cparams(("arbitrary",)),
        name="combine",
    )(dest_flat, dest_flat, h, gates, y)


def _layer(x_prompt, x_sample, cache_k, cache_v, page_table, st_c, st_n, st_m, st_conv,
           norm_mix_g, w_in, b_gates, q_norm_g, k_norm_g, sb_bias, conv_w, conv_b,
           sb_out_g, ml_out_g, w_out, norm_ffn_g, w_router, b_router, w_up, b_up, w_down, b_down):
    bp, sp, d = x_prompt.shape
    bs, ss, _ = x_sample.shape
    assert ss == 1, "decode kernels are written for one new token per sequence"
    tp, ts = bp * sp, bs * ss
    n_main = 3 * SB_WIDTH + 4 * ML_WIDTH

    w_main = w_in[:, :n_main].astype(BF16)
    wg = w_in[:, n_main:]
    w_gate = jnp.pad(wg, ((0, 0), (0, LANES - GATE_LANES))).astype(BF16)
    w_gate_t = jnp.pad(wg.T, ((0, GATE_LANES), (0, 0))).astype(BF16)
    b_gate = jnp.pad(b_gates, (0, LANES - GATE_LANES)).reshape(1, LANES)
    b_gate_t = jnp.pad(b_gates, (0, GATE_LANES)).reshape(2 * GATE_LANES, 1)
    slab = 2 * LANES
    bd = jnp.kron(jnp.eye(slab // SB_HEAD_DIM, dtype=F32),
                  jnp.full((SB_HEAD_DIM, SB_HEAD_DIM), 1.0 / SB_HEAD_DIM, F32)).astype(BF16)
    qk_gain = jnp.concatenate([jnp.tile(q_norm_g, SB_HEADS), jnp.tile(k_norm_g, SB_HEADS)]).reshape(1, -1)
    norm_g = norm_mix_g.reshape(1, d)

    def suffix_ones(n):
        r = jnp.arange(n)
        return (r[:, None] >= r[None, :])

    proj = functools.partial(_proj, norm_g=norm_g, w_main=w_main, w_gate=w_gate, w_gate_t=w_gate_t,
                             bd=bd, qk_gain=qk_gain, b_gate=b_gate, b_gate_t=b_gate_t)
    (q_p, kt_p, kb_p, vt_p, vb_p, mqk_p, mv_p, so_p, gate_p, gatet_p) = proj(
        x_prompt.reshape(tp, d), rows=PROJ_ROWS, seq_len=sp)
    k_p = jnp.transpose(kt_p, (0, 3, 1, 2))
    v_p = jnp.transpose(vt_p, (0, 3, 1, 2))
    (q_s, k_s, _, v_s, _, mqk_s, mv_s, so_s, gate_s, _) = proj(x_sample.reshape(ts, d), rows=ts)

    u_blk = suffix_ones(SB_BLOCK).astype(BF16)
    sb_gain = sb_out_g.reshape(SB_WIDTH // LANES, 1, LANES)
    mix_sb_p = _sb_prompt(sb_bias, q_p.reshape(bp, sp, SB_WIDTH), kb_p.reshape(bp, sp, SB_WIDTH),
                          vb_p.reshape(bp, sp, SB_WIDTH), u_blk, sb_gain)
    tril = suffix_ones(ML_CHUNK).astype(F32)
    triu = tril.T
    ml_gain = ml_out_g.reshape(1, ML_WIDTH)
    conv_b2 = conv_b.reshape(1, 2 * ML_WIDTH)
    gatet_chunks = gatet_p.reshape(2 * GATE_LANES, tp // ML_CHUNK, ML_CHUNK).transpose(1, 0, 2)
    mix_ml_p, c_p, n_p, m_p = _mlstm_prompt(
        mqk_p.reshape(bp, sp, 2 * ML_WIDTH), mv_p.reshape(bp, sp, ML_WIDTH),
        so_p.reshape(bp, sp, ML_WIDTH), gate_p.reshape(bp, sp, LANES), gatet_chunks,
        conv_w, conv_b2, ml_gain, triu, tril)
    conv_p = mqk_p.reshape(bp, sp, 2 * ML_WIDTH)[:, sp - (CONV_WIDTH - 1):]

    page = cache_k.shape[1]
    cache_kt = jnp.transpose(cache_k, (0, 2, 3, 1))
    cache_vt = jnp.transpose(cache_v, (0, 2, 3, 1))
    bias_col = jnp.pad(sb_bias, (0, 2 * SUBLANES - SB_HEADS)).reshape(2 * SUBLANES, 1)
    mix_sb_s = _sb_decode(page_table, q_s.reshape(bs, 1, SB_WIDTH), bias_col,
                          suffix_ones(page).astype(BF16), sb_out_g.reshape(1, SB_WIDTH), cache_kt, cache_vt)
    eye = jnp.eye(ML_HEAD_DIM, dtype=F32)
    mix_ml_s, c_s, n_s, m_s, conv_s = _mlstm_sample(
        mqk_s.reshape(bs, 1, 2 * ML_WIDTH), st_conv, mv_s.reshape(bs, 1, ML_WIDTH),
        so_s.reshape(bs, 1, ML_WIDTH), gate_s.reshape(bs, 1, LANES),
        st_c, st_n.reshape(bs, ML_HEADS, 1, ML_HEAD_DIM), st_m.reshape(bs, ML_HEADS, 1, 1),
        conv_w, conv_b2, ml_gain, eye)

    wr = jnp.pad(w_router, ((0, 0), (0, LANES - N_EXPERTS)))
    wr_hi = wr.astype(BF16)
    wr_lo = (wr - wr_hi.astype(F32)).astype(BF16)
    br = jnp.pad(b_router, (0, LANES - N_EXPERTS)).reshape(1, LANES)
    r = jnp.arange(ROUTER_ROWS)
    tri_strict = (r[None, :] < r[:, None]).astype(BF16)
    h_all, xn_all, idx, gates, rank, counts = _mix_router(
        x_prompt.reshape(tp, d), mix_sb_p.reshape(tp, SB_WIDTH), mix_ml_p.reshape(tp, ML_WIDTH),
        x_sample.reshape(ts, d), mix_sb_s.reshape(ts, SB_WIDTH), mix_ml_s.reshape(ts, ML_WIDTH),
        w_out.astype(BF16), norm_ffn_g.reshape(1, d), wr_hi, wr_lo, br, tri_strict)

    t_all = tp + ts
    rows = EXPERT_ROWS
    cnt = counts[0, :N_EXPERTS].astype(jnp.int32)
    nblk_e = (cnt + rows - 1) // rows
    blk_end = jnp.cumsum(nblk_e)
    blk_start = blk_end - nblk_e
    n_blocks = -(-t_all * TOP_K // rows) + N_EXPERTS
    dest = (blk_start[idx] * rows + rank).reshape(-1)
    step = max(DISPATCH_ROWS, COMBINE_ROWS) * TOP_K
    dest = jnp.pad(dest, (0, -dest.shape[0] % step))
    bid = jnp.arange(n_blocks, dtype=jnp.int32)
    n_used = blk_end[-1]
    bclamp = jnp.minimum(bid, n_used - 1)
    blk_exp = jnp.minimum(jnp.sum(bclamp[:, None] >= blk_end[None, :], axis=1), N_EXPERTS - 1).astype(jnp.int32)
    cap = n_blocks * rows

    xbuf = _dispatch(dest, xn_all, jnp.zeros((cap, d // 2), jnp.uint32))
    y = _experts(bclamp, blk_exp, n_used.reshape(1), xbuf,
                 w_up.astype(BF16), b_up.reshape(N_EXPERTS, 1, -1),
                 w_down.astype(BF16), b_down.reshape(N_EXPERTS, 1, -1))
    out_p, out_s = _combine(dest, h_all, gates, y, tp)

    y_prompt = out_p.reshape(bp, sp, d)
    y_sample = out_s.reshape(bs, ss, d)
    return (y_prompt, y_sample,
            k_p, v_p,
            k_s.reshape(bs, ss, SB_HEADS, SB_HEAD_DIM), v_s.reshape(bs, ss, SB_HEADS, SB_HEAD_DIM),
            c_p, n_p.reshape(bp, ML_HEADS, ML_HEAD_DIM), m_p.reshape(bp, ML_HEADS), conv_p,
            c_s, n_s.reshape(bs, ML_HEADS, ML_HEAD_DIM), m_s.reshape(bs, ML_HEADS), conv_s)


def kernel(x_prompt, x_sample, cache_sb_k, cache_sb_v, page_table, state_mlstm_C, state_mlstm_n, state_mlstm_m, state_mlstm_conv, norm_mix_g, w_in, b_gates, q_norm_g, k_norm_g, sb_bias, conv_w, conv_b, sb_out_g, ml_out_g, w_out, norm_ffn_g, w_router, b_router, w_up, b_up, w_down, b_down):
    depth = w_in.shape[0]
    assert depth == 1, "single-layer step"
    outs = _layer(x_prompt, x_sample, cache_sb_k[0], cache_sb_v[0], page_table,
                  state_mlstm_C[0], state_mlstm_n[0], state_mlstm_m[0], state_mlstm_conv[0],
                  norm_mix_g[0], w_in[0], b_gates[0], q_norm_g[0], k_norm_g[0], sb_bias[0],
                  conv_w[0], conv_b[0], sb_out_g[0], ml_out_g[0], w_out[0], norm_ffn_g[0],
                  w_router[0], b_router[0], w_up[0], b_up[0], w_down[0], b_down[0])
    return tuple(o[None] if i >= 2 else o for i, o in enumerate(outs))
```
